```python
import jax
import jax.numpy as jnp
from jax import lax
import numpy as np

D_MODEL = 2048
BATCH = 1
SEQ = 16384
DEPTH = 1

N_MEM = 256
HEAD_DIM = 128
EPS = 1e-6
N_HEADS_A = (D_MODEL // 2) // HEAD_DIM
DILATED_PATTERNS = ((128, 1), (512, 4), (2048, 16))
ATTN_BLOCK = 128
N_HEADS_B = (D_MODEL // 2) // HEAD_DIM
DK_B = HEAD_DIM
DV_B = HEAD_DIM
CONV_WIDTH = 4
DELTA_CHUNK = 64
W_A = N_HEADS_A * HEAD_DIM
W_BK = N_HEADS_B * DK_B
W_BV = N_HEADS_B * DV_B
MIX_WIDTH = W_A + W_BV
D_IN = 3 * W_A + 2 * W_BK + 2 * W_BV + 2 * N_HEADS_B
N_HEADS_MEM = 4
N_GROUPS = 8
EXPERTS_PER_GROUP = 8
N_EXPERTS = N_GROUPS * EXPERTS_PER_GROUP
TOP_K_EXPERT = 2
D_EXPERT = D_MODEL // 4
MOE_BLOCK = 128

kernel_name = 'hybrid_dilated_deltanet_hmoe'


def rms_norm(x, g):
    x32 = x.astype(jnp.float32)
    return x32 * lax.rsqrt(jnp.mean(x32 * x32, axis=-1, keepdims=True) + EPS) * g.astype(jnp.float32)


def l2_normalize(x):
    return x * lax.rsqrt(jnp.sum(x * x, axis=-1, keepdims=True) + EPS)


def alibi_slopes(n_heads):
    return jnp.asarray(2.0 ** (-8.0 * np.arange(1, n_heads + 1) / n_heads), dtype=jnp.float32)


def dilated_window_attn(q, k, v, slopes, window, dilation):
    B, S, H, Dh = q.shape
    d = dilation
    n_back = window // d
    L = S // d
    nb = -(-L // ATTN_BLOCK)
    Lp = nb * ATTN_BLOCK
    Z = B * d

    def by_residue(t):
        t = t.reshape(B, L, d, H, Dh).transpose(0, 2, 3, 1, 4).reshape(Z, H, L, Dh)
        return jnp.pad(t, ((0, 0), (0, 0), (0, Lp - L), (0, 0)))

    def band(t):
        tb = t.reshape(Z, H, nb, ATTN_BLOCK, Dh)
        prev = jnp.pad(tb, ((0, 0), (0, 0), (1, 0), (0, 0), (0, 0)))[:, :, :-1]
        return jnp.concatenate([prev, tb], axis=3)

    qb = by_residue(q).reshape(Z, H, nb, ATTN_BLOCK, Dh)
    kb = band(by_residue(k))
    vb = band(by_residue(v))
    s = jnp.einsum('zhnqd,zhnkd->zhnqk', qb, kb) * (Dh ** -0.5)
    qi = jnp.arange(ATTN_BLOCK)[:, None] + ATTN_BLOCK
    ki = jnp.arange(2 * ATTN_BLOCK)[None, :]
    dist = qi - ki
    key_idx = (jnp.arange(nb) * ATTN_BLOCK - ATTN_BLOCK)[:, None, None] + ki[None]
    valid = (dist >= 0) & (dist <= n_back) & (key_idx >= 0)
    bias = -slopes[:, None, None, None] * (d * dist).astype(jnp.float32)
    s = jnp.where(valid, s + bias, -jnp.inf)
    lse = jax.nn.logsumexp(s, axis=-1)
    p = jnp.exp(s - lse[..., None])
    o = jnp.einsum('zhnqk,zhnkd->zhnqd', p, vb)
    o = o.reshape(Z, H, Lp, Dh)[:, :, :L].reshape(B, d, H, L, Dh).transpose(0, 3, 1, 2, 4).reshape(B, S, H, Dh)
    lse = lse.reshape(Z, H, Lp)[:, :, :L].reshape(B, d, H, L).transpose(0, 3, 1, 2).reshape(B, S, H)
    return o, lse


def dilated_attention(q, k, v):
    slopes = alibi_slopes(q.shape[2])
    outs = []
    lses = []
    for window, dilation in DILATED_PATTERNS:
        o, lse = dilated_window_attn(q, k, v, slopes, window, dilation)
        outs.append(o)
        lses.append(lse)
    w = jax.nn.softmax(jnp.stack(lses, axis=0), axis=0)
    return jnp.sum(w[..., None] * jnp.stack(outs, axis=0), axis=0)


def causal_depthwise_conv(u, w):
    C = u.shape[-1]
    return lax.conv_general_dilated(u, w[:, None, :].astype(u.dtype), (1,), [(CONV_WIDTH - 1, 0)],
                                    dimension_numbers=('NWC', 'WIO', 'NWC'), feature_group_count=C)


def gated_delta_rule(q, k, v, g, beta):
    B, S, H, dk = q.shape
    dv = v.shape[-1]
    C = DELTA_CHUNK
    N = S // C

    def chunks(t):
        return jnp.moveaxis(t.reshape((B, N, C, H) + t.shape[3:]), 3, 1)

    q = chunks(q * (dk ** -0.5))
    k = chunks(k)
    v = chunks(v)
    beta = chunks(beta)
    g = jnp.cumsum(chunks(g), axis=-1)
    causal = jnp.tril(jnp.ones((C, C), dtype=bool))
    strict = jnp.tril(jnp.ones((C, C), dtype=bool), -1)
    decay = jnp.exp(jnp.where(causal, g[..., :, None] - g[..., None, :], -jnp.inf))
    kb = k * beta[..., None]
    a_mat = jnp.where(strict, jnp.einsum('bhncd,bhnsd->bhncs', kb, k) * decay, 0.0)
    eye = jnp.eye(C, dtype=a_mat.dtype)
    t_mat = lax.linalg.triangular_solve(eye + a_mat, jnp.broadcast_to(eye, a_mat.shape),
                                        left_side=True, lower=True, unit_diagonal=True)
    u = jnp.einsum('bhncs,bhnsv->bhncv', t_mat, v * beta[..., None])
    w = jnp.einsum('bhncs,bhnsk->bhnck', t_mat, kb * jnp.exp(g)[..., None])
    qk = jnp.einsum('bhncd,bhnsd->bhncs', q, k) * decay
    g_last = g[..., -1]
    q_dec = q * jnp.exp(g)[..., None]
    k_dec = k * jnp.exp(g_last[..., None] - g)[..., None]

    def step(state, inp):
        q_i, k_i, u_i, w_i, qk_i, gl_i = inp
        v_new = u_i - jnp.einsum('bhck,bhkv->bhcv', w_i, state)
        o_i = jnp.einsum('bhck,bhkv->bhcv', q_i, state) + jnp.einsum('bhcs,bhsv->bhcv', qk_i, v_new)
        state = state * jnp.exp(gl_i)[..., None, None] + jnp.einsum('bhck,bhcv->bhkv', k_i, v_new)
        return state, o_i

    xs = (jnp.moveaxis(q_dec, 2, 0), jnp.moveaxis(k_dec, 2, 0), jnp.moveaxis(u, 2, 0),
          jnp.moveaxis(w, 2, 0), jnp.moveaxis(qk, 2, 0), jnp.moveaxis(g_last, 2, 0))
    state0 = jnp.zeros((B, H, dk, dv), dtype=jnp.float32)
    _, o = lax.scan(step, state0, xs)
    return o.transpose(1, 0, 3, 2, 4).reshape(B, S, H, dv)


def gated_deltanet(qkv, z, b, a, conv_w, a_log, dt_bias, g_out):
    B, S, _ = qkv.shape
    qkv = jax.nn.silu(causal_depthwise_conv(qkv, conv_w))
    q, k, v = jnp.split(qkv, [W_BK, 2 * W_BK], axis=-1)
    q = l2_normalize(q.reshape(B, S, N_HEADS_B, DK_B))
    k = l2_normalize(k.reshape(B, S, N_HEADS_B, DK_B))
    v = v.reshape(B, S, N_HEADS_B, DV_B)
    beta = jax.nn.sigmoid(b)
    g = -jnp.exp(a_log.astype(jnp.float32)) * jax.nn.softplus(a + dt_bias)
    o = gated_delta_rule(q, k, v, g, beta)
    o = rms_norm(o, g_out) * jax.nn.silu(z.reshape(B, S, N_HEADS_B, DV_B))
    return o.reshape(B, S, W_BV)


def memory_cross_attention(h, mem_n, w_q, w_kv, w_o):
    B, S, _ = h.shape
    M = mem_n.shape[1]
    q = (h @ w_q).reshape(B, S, N_HEADS_MEM, HEAD_DIM)
    k, v = jnp.split(mem_n @ w_kv, 2, axis=-1)
    k = k.reshape(B, M, N_HEADS_MEM, HEAD_DIM)
    v = v.reshape(B, M, N_HEADS_MEM, HEAD_DIM)
    p = jax.nn.softmax(jnp.einsum('bshd,bmhd->bhsm', q, k) * (HEAD_DIM ** -0.5), axis=-1)
    o = jnp.einsum('bhsm,bmhd->bshd', p, v).reshape(B, S, N_HEADS_MEM * HEAD_DIM)
    return o @ w_o


def hierarchical_moe(h, w_group, b_group, w_expert, b_expert, w_gate, w_up, w_down):
    B, S, D = h.shape
    T = B * S
    t = h.reshape(T, D)
    g_logits = t @ w_group + b_group
    g_sel = jnp.argmax(g_logits, axis=-1)
    g_gate = jnp.take_along_axis(jax.nn.softmax(g_logits, axis=-1), g_sel[:, None], axis=1)
    e_logits = (t @ w_expert + b_expert).reshape(T, N_GROUPS, EXPERTS_PER_GROUP)
    e_logits = jnp.take_along_axis(e_logits, g_sel[:, None, None], axis=1)[:, 0]
    top_v, top_i = lax.top_k(e_logits, TOP_K_EXPERT)
    weights = g_gate * jax.nn.softmax(top_v, axis=-1)
    expert_id = g_sel[:, None] * EXPERTS_PER_GROUP + top_i

    M = T * TOP_K_EXPERT
    e_flat = expert_id.reshape(M)
    w_flat = weights.reshape(M)
    tok_flat = jnp.repeat(jnp.arange(T, dtype=jnp.int32), TOP_K_EXPERT)
    order = jnp.argsort(e_flat)
    e_sorted = e_flat[order]
    counts = jnp.bincount(e_flat, length=N_EXPERTS)
    padded = ((counts + MOE_BLOCK - 1) // MOE_BLOCK) * MOE_BLOCK
    start = jnp.cumsum(counts) - counts
    pstart = jnp.cumsum(padded) - padded
    dest = pstart[e_sorted] + jnp.arange(M) - start[e_sorted]
    P = (-(-M // MOE_BLOCK)) * MOE_BLOCK + N_EXPERTS * MOE_BLOCK
    n_blocks = P // MOE_BLOCK
    buf_tok = jnp.full((P,), T, dtype=jnp.int32).at[dest].set(tok_flat[order])
    buf_w = jnp.zeros((P,), dtype=t.dtype).at[dest].set(w_flat[order])
    block_expert = jnp.minimum(jnp.searchsorted(jnp.cumsum(padded), jnp.arange(n_blocks) * MOE_BLOCK, side='right'),
                               N_EXPERTS - 1)
    t_pad = jnp.concatenate([t, jnp.zeros((1, D), dtype=t.dtype)], axis=0)
    xb = t_pad[buf_tok].reshape(n_blocks, MOE_BLOCK, D)

    def expert_block(args):
        x_blk, e = args
        hid = jax.nn.silu(x_blk @ w_gate[e]) * (x_blk @ w_up[e])
        return hid @ w_down[e]

    yb = lax.map(expert_block, (xb, block_expert)).reshape(P, D)
    y = jnp.zeros((T + 1, D), dtype=yb.dtype).at[buf_tok].add(yb * buf_w[:, None])[:T]
    return y.reshape(B, S, D)


def setup_inputs(seed: int = 0) -> dict:
    key = jax.random.key(seed)
    ks = jax.random.split(key, 26)
    L = DEPTH
    f32 = jnp.float32

    def nrm(k, shape, scale):
        return jax.random.normal(k, shape, f32) * scale

    def gain(k, shape):
        return 1.0 + 0.02 * jax.random.normal(k, shape, f32)

    dt = jnp.exp(jax.random.uniform(ks[6], (L, N_HEADS_B), f32, minval=np.log(1e-3), maxval=np.log(1e-1)))
    return {
        'x': nrm(ks[0], (BATCH, SEQ, D_MODEL), 1.0),
        'mem': nrm(ks[1], (BATCH, N_MEM, D_MODEL), 1.0),
        'g_mix': gain(ks[2], (L, D_MODEL)),
        'w_in': nrm(ks[3], (L, D_MODEL, D_IN), D_MODEL ** -0.5),
        'conv_w': nrm(ks[4], (L, CONV_WIDTH, W_BK + W_BK + W_BV), CONV_WIDTH ** -0.5),
        'a_log': jnp.log(jax.random.uniform(ks[5], (L, N_HEADS_B), f32, minval=1.0, maxval=16.0)),
        'dt_bias': dt + jnp.log(-jnp.expm1(-dt)),
        'g_delta_out': gain(ks[7], (L, DV_B)),
        'g_attn_out': gain(ks[8], (L, W_A)),
        'w_out': nrm(ks[9], (L, MIX_WIDTH, D_MODEL), MIX_WIDTH ** -0.5),
        'g_cross': gain(ks[10], (L, D_MODEL)),
        'g_mem': gain(ks[11], (L, D_MODEL)),
        'w_q_mem': nrm(ks[12], (L, D_MODEL, N_HEADS_MEM * HEAD_DIM), D_MODEL ** -0.5),
        'w_kv_mem': nrm(ks[13], (L, D_MODEL, 2 * N_HEADS_MEM * HEAD_DIM), D_MODEL ** -0.5),
        'w_o_mem': nrm(ks[14], (L, N_HEADS_MEM * HEAD_DIM, D_MODEL), (N_HEADS_MEM * HEAD_DIM) ** -0.5),
        'g_moe': gain(ks[15], (L, D_MODEL)),
        'w_group': nrm(ks[16], (L, D_MODEL, N_GROUPS), D_MODEL ** -0.5),
        'b_group': nrm(ks[17], (L, N_GROUPS), 0.01),
        'w_expert': nrm(ks[18], (L, D_MODEL, N_EXPERTS), D_MODEL ** -0.5),
        'b_expert': nrm(ks[19], (L, N_EXPERTS), 0.01),
        'w_gate': nrm(ks[20], (L, N_EXPERTS, D_MODEL, D_EXPERT), D_MODEL ** -0.5),
        'w_up': nrm(ks[21], (L, N_EXPERTS, D_MODEL, D_EXPERT), D_MODEL ** -0.5),
        'w_down': nrm(ks[22], (L, N_EXPERTS, D_EXPERT, D_MODEL), D_EXPERT ** -0.5),
        'g_final': gain(ks[23], (D_MODEL,)),
    }


def reference(x, mem, g_mix, w_in, conv_w, a_log, dt_bias, g_delta_out, g_attn_out, w_out,
              g_cross, g_mem, w_q_mem, w_kv_mem, w_o_mem,
              g_moe, w_group, b_group, w_expert, b_expert, w_gate, w_up, w_down, g_final):
    in_dtype = x.dtype
    h = x.astype(jnp.float32)
    B, S, _ = h.shape
    splits = [W_A, 2 * W_A, 3 * W_A, 3 * W_A + 2 * W_BK + W_BV, 3 * W_A + 2 * W_BK + 2 * W_BV,
              3 * W_A + 2 * W_BK + 2 * W_BV + N_HEADS_B]
    for l in range(DEPTH):
        u = rms_norm(h, g_mix[l])
        proj = u @ w_in[l]
        qa, ka, va, qkv_d, z_d, b_d, a_d = jnp.split(proj, splits, axis=-1)
        attn = dilated_attention(qa.reshape(B, S, N_HEADS_A, HEAD_DIM),
                                 ka.reshape(B, S, N_HEADS_A, HEAD_DIM),
                                 va.reshape(B, S, N_HEADS_A, HEAD_DIM))
        attn = rms_norm(attn.reshape(B, S, W_A), g_attn_out[l])
        delta = gated_deltanet(qkv_d, z_d, b_d, a_d, conv_w[l], a_log[l], dt_bias[l], g_delta_out[l])
        h = h + jnp.concatenate([attn, delta], axis=-1) @ w_out[l]
        mem_n = rms_norm(mem, g_mem[l])
        h = h + memory_cross_attention(rms_norm(h, g_cross[l]), mem_n, w_q_mem[l], w_kv_mem[l], w_o_mem[l])
        h = h + hierarchical_moe(rms_norm(h, g_moe[l]), w_group[l], b_group[l], w_expert[l], b_expert[l],
                                 w_gate[l], w_up[l], w_down[l])
    return rms_norm(h, g_final).astype(in_dtype)
```

```python
import functools

import numpy as np
import jax
import jax.numpy as jnp
from jax import lax
from jax.experimental import pallas as pl
from jax.experimental.pallas import tpu as pltpu

F32 = jnp.float32
BF16 = jnp.bfloat16
I32 = jnp.int32
HIGHEST = lax.Precision.HIGHEST

EPS = 1e-6
HEAD_DIM = 128
LANES = 128
SUBLANES = 8
NEG = -1e30

N_HEADS_A = 8
N_HEADS_B = 8
W_A = N_HEADS_A * HEAD_DIM
W_B = N_HEADS_B * HEAD_DIM
W_MAIN = 3 * W_A + 4 * W_B
DILATED_PATTERNS = ((128, 1), (512, 4), (2048, 16))
ATTN_BLOCK = 128
ATTN_TILE = 2048
CONV_WIDTH = 4
DELTA_CHUNK = 64
DELTA_BLOCK = 256
N_MEM_HEADS = 4
W_MEM = N_MEM_HEADS * HEAD_DIM
N_GROUPS = 8
EXPERTS_PER_GROUP = 8
N_EXPERTS = 64
TOP_K = 2
MOE_BLOCK = 128

VMEM_LIMIT = 56 * 1024 * 1024


def _cparams(sem):
    return pltpu.CompilerParams(dimension_semantics=sem, vmem_limit_bytes=VMEM_LIMIT)


def _rms(x, g):
    return x * lax.rsqrt(jnp.mean(x * x, axis=-1, keepdims=True) + EPS) * g


def _dot(a, b):
    return jnp.dot(a.astype(BF16), b.astype(BF16), preferred_element_type=F32)


def _dot_nt(a, b):
    return lax.dot_general(a.astype(BF16), b.astype(BF16), (((1,), (1,)), ((), ())), preferred_element_type=F32)


def _dot_tn(a, b):
    return lax.dot_general(a.astype(BF16), b.astype(BF16), (((0,), (0,)), ((), ())), preferred_element_type=F32)


def _inproj_kernel(x_ref, g_ref, w_ref, ws_ref, o_ref, os_ref, u_scr):
    @pl.when(pl.program_id(1) == 0)
    def _():
        u = _rms(x_ref[...], g_ref[...])
        u_scr[...] = u.astype(BF16)
        os_ref[...] = jnp.dot(u, ws_ref[...], precision=HIGHEST, preferred_element_type=F32)

    o_ref[...] = jnp.dot(u_scr[...], w_ref[...], preferred_element_type=F32)


def _inproj(x, g, w_main, w_small, tm=512, tn=1024):
    S, D = x.shape
    N = w_main.shape[1]
    return pl.pallas_call(
        _inproj_kernel,
        grid=(S // tm, N // tn),
        in_specs=[pl.BlockSpec((tm, D), lambda i, j: (i, 0)),
                  pl.BlockSpec((1, D), lambda i, j: (0, 0)),
                  pl.BlockSpec((D, tn), lambda i, j: (0, j)),
                  pl.BlockSpec((D, LANES), lambda i, j: (0, 0))],
        out_specs=[pl.BlockSpec((tm, tn), lambda i, j: (i, j)),
                   pl.BlockSpec((tm, LANES), lambda i, j: (i, 0))],
        out_shape=[jax.ShapeDtypeStruct((S, N), F32), jax.ShapeDtypeStruct((S, LANES), F32)],
        scratch_shapes=[pltpu.VMEM((tm, D), BF16)],
        compiler_params=_cparams(("parallel", "arbitrary")),
        name="inproj",
    )(x, g, w_main, w_small)


def _attn_kernel(slope_ref, q_ref, kp_ref, kc_ref, vp_ref, vc_ref, o_ref, kk, vv, o_scr, l_scr):
    i = pl.program_id(0)
    h = pl.program_id(1)
    T = ATTN_TILE
    B = ATTN_BLOCK
    slope = slope_ref[h]
    kk[0:T, :] = kp_ref[...]
    kk[T:2 * T, :] = kc_ref[...]
    vv[0:T, :] = vp_ref[...]
    vv[T:2 * T, :] = vc_ref[...]
    qi = lax.broadcasted_iota(I32, (B, 2 * B), 0)
    ki = lax.broadcasted_iota(I32, (B, 2 * B), 1)
    dist = qi + B - ki
    band = (dist >= 0) & (dist <= B)
    band_first = band & (ki >= B)
    distf = dist.astype(F32)
    scale = HEAD_DIM ** -0.5

    for p, (window, d) in enumerate(DILATED_PATTERNS):
        assert window // d == B and T % (B * d) == 0
        bias = jnp.where(band, distf * (-slope * d), NEG)
        bias_first = jnp.where(band_first, distf * (-slope * d), NEG)

        def body(b, carry, p=p, d=d, bias=bias, bias_first=bias_first):
            r = b % d
            j = b // d
            qs = r + B * d * j
            ks = T - B * d + qs
            q = q_ref[pl.ds(qs, B, stride=d), :] * scale
            k = kk[pl.ds(ks, 2 * B, stride=d), :]
            v = vv[pl.ds(ks, 2 * B, stride=d), :]
            s = _dot_nt(q, k)
            first = jnp.logical_and(i == 0, j == 0)
            s = s + jnp.where(first, bias_first, bias)
            m = jnp.max(s, axis=-1, keepdims=True)
            e = jnp.exp(s - m)
            l = jnp.sum(e, axis=-1, keepdims=True)
            o = _dot(e, v) / l
            lse = m + jnp.log(l)
            o_scr[p, pl.ds(qs, B, stride=d), :] = o
            l_scr[p, pl.ds(qs, B, stride=d), :] = jnp.broadcast_to(lse, (B, HEAD_DIM))
            return carry

        lax.fori_loop(0, T // B, body, 0)

    l0, l1, l2 = l_scr[0], l_scr[1], l_scr[2]
    m = jnp.maximum(jnp.maximum(l0, l1), l2)
    w0, w1, w2 = jnp.exp(l0 - m), jnp.exp(l1 - m), jnp.exp(l2 - m)
    o_ref[...] = (w0 * o_scr[0] + w1 * o_scr[1] + w2 * o_scr[2]) / (w0 + w1 + w2)


def _dilated_attention(proj, slopes):
    S = proj.shape[0]
    T = ATTN_TILE
    H = N_HEADS_A
    blk = (T, HEAD_DIM)
    return pl.pallas_call(
        _attn_kernel,
        grid_spec=pltpu.PrefetchScalarGridSpec(
            num_scalar_prefetch=1,
            grid=(S // T, H),
            in_specs=[pl.BlockSpec(blk, lambda i, h, s: (i, h)),
                      pl.BlockSpec(blk, lambda i, h, s: (jnp.maximum(i - 1, 0), H + h)),
                      pl.BlockSpec(blk, lambda i, h, s: (i, H + h)),
                      pl.BlockSpec(blk, lambda i, h, s: (jnp.maximum(i - 1, 0), 2 * H + h)),
                      pl.BlockSpec(blk, lambda i, h, s: (i, 2 * H + h))],
            out_specs=pl.BlockSpec(blk, lambda i, h, s: (i, h)),
            scratch_shapes=[pltpu.VMEM((2 * T, HEAD_DIM), F32), pltpu.VMEM((2 * T, HEAD_DIM), F32),
                            pltpu.VMEM((3, T, HEAD_DIM), F32), pltpu.VMEM((3, T, HEAD_DIM), F32)]),
        out_shape=jax.ShapeDtypeStruct((S, W_A), F32),
        compiler_params=_cparams(("parallel", "parallel")),
        name="attn",
    )(slopes, proj, proj, proj, proj, proj)


def _softplus(x):
    return jnp.maximum(x, 0.0) + jnp.log1p(jnp.exp(-jnp.abs(x)))


def _delta_kernel(hist_ref, qkv_ref, z_ref, ba_ref, cw_ref, alog_ref, dtb_ref, gout_ref, o_ref,
                  ext, act, gct, state):
    i = pl.program_id(0)
    CB = DELTA_BLOCK
    C = DELTA_CHUNK
    H = N_HEADS_B
    D = HEAD_DIM

    @pl.when(i == 0)
    def _():
        state[...] = jnp.zeros_like(state)
        ext[0:SUBLANES, :] = jnp.zeros((SUBLANES, 3 * W_B), F32)

    @pl.when(i > 0)
    def _():
        ext[0:SUBLANES, :] = hist_ref[...]

    ext[SUBLANES:SUBLANES + CB, :] = qkv_ref[...]

    for c in range(3 * H):
        sl = slice(c * D, (c + 1) * D)
        acc = cw_ref[CONV_WIDTH - 1:CONV_WIDTH, sl] * ext[SUBLANES:SUBLANES + CB, sl]
        for t in range(1, CONV_WIDTH):
            acc = acc + cw_ref[CONV_WIDTH - 1 - t:CONV_WIDTH - t, sl] * ext[SUBLANES - t:SUBLANES - t + CB, sl]
        a = acc * jax.nn.sigmoid(acc)
        if c < 2 * H:
            a = a * lax.rsqrt(jnp.sum(a * a, axis=-1, keepdims=True) + EPS)
        if c < H:
            a = a * (D ** -0.5)
        act[:, sl] = a

    ba = ba_ref[...]
    beta = jax.nn.sigmoid(ba)
    g = -jnp.exp(alog_ref[...]) * _softplus(ba + dtb_ref[...])
    row = lax.broadcasted_iota(I32, (CB, CB), 0)
    col = lax.broadcasted_iota(I32, (CB, CB), 1)
    chunk_tri = jnp.where((row // C == col // C) & (row >= col), 1.0, 0.0).astype(F32)
    gc = jnp.dot(chunk_tri, g, precision=HIGHEST, preferred_element_type=F32)
    gct[...] = gc.T
    eg = jnp.exp(gc)

    ri = lax.broadcasted_iota(I32, (C, C), 0)
    ci = lax.broadcasted_iota(I32, (C, C), 1)
    causal = ri >= ci
    strict = ri > ci
    gout = gout_ref[...]

    for n in range(CB // C):
        rows = slice(n * C, (n + 1) * C)
        for h in range(H):
            q = act[rows, h * D:(h + 1) * D]
            k = act[rows, (H + h) * D:(H + h + 1) * D]
            v = act[rows, (2 * H + h) * D:(2 * H + h + 1) * D]
            beta_c = beta[rows, h:h + 1]
            gc_c = gc[rows, H + h:H + h + 1]
            eg_c = eg[rows, H + h:H + h + 1]
            gc_r = gct[H + h:H + h + 1, rows]
            g_last = gc_r[:, C - 1:C]
            decay = jnp.exp(jnp.where(causal, gc_c - gc_r, NEG))
            kb = k * beta_c
            a_mat = jnp.where(strict, _dot_nt(kb, k) * decay, 0.0)
            x = jnp.concatenate([v * beta_c, kb * eg_c], axis=-1)
            x = x - _dot(a_mat, x)
            pw = a_mat
            for _ in range(int(np.log2(C)) - 1):
                pw = _dot(pw, pw)
                x = x + _dot(pw, x)
            u = x[:, :D]
            w = x[:, D:]
            qk = jnp.where(causal, _dot_nt(q, k) * decay, 0.0)
            s_h = state[h]
            v_new = u - _dot(w, s_h)
            o = _dot(q * eg_c, s_h) + _dot(qk, v_new)
            state[h] = s_h * jnp.exp(g_last) + _dot_tn(k * jnp.exp(g_last - gc_c), v_new)
            z = z_ref[rows, h * D:(h + 1) * D]
            o_ref[rows, h * D:(h + 1) * D] = _rms(o, gout) * (z * jax.nn.sigmoid(z))


def _gated_deltanet(proj, ba, conv_w, alog_pad, dtb_pad, g_out):
    S = proj.shape[0]
    CB = DELTA_BLOCK
    qkv_col = W_A * 3 // (3 * W_B)
    z_col = (3 * W_A + 3 * W_B) // W_B
    return pl.pallas_call(
        _delta_kernel,
        grid=(S // CB,),
        in_specs=[pl.BlockSpec((SUBLANES, 3 * W_B), lambda i: (jnp.maximum(i * (CB // SUBLANES) - 1, 0), qkv_col)),
                  pl.BlockSpec((CB, 3 * W_B), lambda i: (i, qkv_col)),
                  pl.BlockSpec((CB, W_B), lambda i: (i, z_col)),
                  pl.BlockSpec((CB, LANES), lambda i: (i, 0)),
                  pl.BlockSpec((CONV_WIDTH, 3 * W_B), lambda i: (0, 0)),
                  pl.BlockSpec((1, LANES), lambda i: (0, 0)),
                  pl.BlockSpec((1, LANES), lambda i: (0, 0)),
                  pl.BlockSpec((1, HEAD_DIM), lambda i: (0, 0))],
        out_specs=pl.BlockSpec((CB, W_B), lambda i: (i, 0)),
        out_shape=jax.ShapeDtypeStruct((S, W_B), F32),
        scratch_shapes=[pltpu.VMEM((CB + SUBLANES, 3 * W_B), F32),
                        pltpu.VMEM((CB, 3 * W_B), F32),
                        pltpu.VMEM((LANES, CB), F32),
                        pltpu.VMEM((N_HEADS_B, HEAD_DIM, HEAD_DIM), F32)],
        compiler_params=_cparams(("arbitrary",)),
        name="delta",
    )(proj, proj, proj, ba, conv_w, alog_pad, dtb_pad, g_out)


def _memkv_kernel(mem_ref, g_ref, w_ref, k_ref, v_ref):
    kv = _dot(_rms(mem_ref[...], g_ref[...]), w_ref[...])
    k_ref[...] = kv[:, :W_MEM].astype(BF16)
    v_ref[...] = kv[:, W_MEM:].astype(BF16)


def _memkv(mem, g, w_kv):
    M = mem.shape[0]
    return pl.pallas_call(
        _memkv_kernel,
        out_shape=[jax.ShapeDtypeStruct((M, W_MEM), BF16), jax.ShapeDtypeStruct((M, W_MEM), BF16)],
        compiler_params=pltpu.CompilerParams(vmem_limit_bytes=VMEM_LIMIT),
        name="memkv",
    )(mem, g, w_kv)


def _mix_kernel(attn_ref, delta_ref, x_ref, gattn_ref, wout_ref, gcross_ref, wq_ref, km_ref, vm_ref, wo_ref,
                gmoe_ref, wr_ref, br_ref, h2_ref, hn_ref, eid_ref, wts_ref):
    an = _rms(attn_ref[...], gattn_ref[...])
    mix = jnp.concatenate([an.astype(BF16), delta_ref[...].astype(BF16)], axis=-1)
    h1 = x_ref[...] + jnp.dot(mix, wout_ref[...], preferred_element_type=F32)

    q = _dot(_rms(h1, gcross_ref[...]), wq_ref[...]) * (HEAD_DIM ** -0.5)
    outs = []
    for hh in range(N_MEM_HEADS):
        sl = slice(hh * HEAD_DIM, (hh + 1) * HEAD_DIM)
        s = _dot_nt(q[:, sl], km_ref[:, sl])
        e = jnp.exp(s - jnp.max(s, axis=-1, keepdims=True))
        outs.append(_dot(e, vm_ref[:, sl]) / jnp.sum(e, axis=-1, keepdims=True))
    h2 = h1 + _dot(jnp.concatenate(outs, axis=-1), wo_ref[...])
    h2_ref[...] = h2

    hn = _rms(h2, gmoe_ref[...])
    hn_ref[...] = hn
    logits = jnp.dot(hn, wr_ref[...], precision=HIGHEST, preferred_element_type=F32) + br_ref[...]
    lane = lax.broadcasted_iota(I32, logits.shape, 1)
    gl = jnp.where(lane < N_GROUPS, logits, NEG)
    gmax = jnp.max(gl, axis=-1, keepdims=True)
    g_sel = jnp.min(jnp.where(gl == gmax, lane, LANES), axis=-1, keepdims=True)
    g_gate = 1.0 / jnp.sum(jnp.exp(gl - gmax), axis=-1, keepdims=True)
    in_group = (lane >= N_GROUPS) & ((lane - N_GROUPS) // EXPERTS_PER_GROUP == g_sel)
    el = jnp.where(in_group, logits, NEG)
    v1 = jnp.max(el, axis=-1, keepdims=True)
    i1 = jnp.min(jnp.where(in_group & (el == v1), lane, LANES), axis=-1, keepdims=True)
    in_rest = in_group & (lane != i1)
    el2 = jnp.where(in_rest, logits, NEG)
    v2 = jnp.max(el2, axis=-1, keepdims=True)
    i2 = jnp.min(jnp.where(in_rest & (el2 == v2), lane, LANES), axis=-1, keepdims=True)
    e2 = jnp.exp(v2 - v1)
    w1 = g_gate / (1.0 + e2)
    w2 = g_gate * e2 / (1.0 + e2)
    eid_ref[...] = jnp.where(lane == 0, i1 - N_GROUPS, jnp.where(lane == 1, i2 - N_GROUPS, 0))
    wts_ref[...] = jnp.where(lane == 0, w1, jnp.where(lane == 1, w2, 0.0))


def _mix(attn, delta, x, g_attn, w_out, g_cross, w_q, k_mem, v_mem, w_o, g_moe, w_router, b_router, tm=256):
    S, D = x.shape
    M = k_mem.shape[0]
    row = lambda w: pl.BlockSpec((tm, w), lambda i: (i, 0))
    full = lambda a: pl.BlockSpec(a.shape, lambda i: (0, 0))
    return pl.pallas_call(
        _mix_kernel,
        grid=(S // tm,),
        in_specs=[row(W_A), row(W_B), row(D), full(g_attn), full(w_out), full(g_cross), full(w_q),
                  full(k_mem), full(v_mem), full(w_o), full(g_moe), full(w_router), full(b_router)],
        out_specs=[row(D), row(D), row(LANES), row(LANES)],
        out_shape=[jax.ShapeDtypeStruct((S, D), F32), jax.ShapeDtypeStruct((S, D), F32),
                   jax.ShapeDtypeStruct((S, LANES), I32), jax.ShapeDtypeStruct((S, LANES), F32)],
        compiler_params=_cparams(("parallel",)),
        name="mix",
    )(attn, delta, x, g_attn, w_out, g_cross, w_q, k_mem, v_mem, w_o, g_moe, w_router, b_router)


def _rank_kernel(eid_ref, dest_ref, cnt_ref, carry, pstart):
    ph = pl.program_id(0)
    i = pl.program_id(1)
    tm = eid_ref.shape[0]
    lane = lax.broadcasted_iota(I32, (tm, LANES), 1)
    e = eid_ref[...]
    e1 = e[:, 0:1]
    e2 = e[:, 1:2]
    onehot = jnp.where((lane == e1) | (lane == e2), 1.0, 0.0).astype(F32)
    colsum = jnp.sum(onehot, axis=0, keepdims=True)

    @pl.when(jnp.logical_and(ph == 0, i == 0))
    def _():
        carry[...] = jnp.zeros_like(carry)

    @pl.when(jnp.logical_and(ph == 1, i == 0))
    def _():
        cnt = carry[...]
        cnt_ref[...] = cnt
        padded = jnp.floor((cnt + (MOE_BLOCK - 1)) * (1.0 / MOE_BLOCK)) * MOE_BLOCK
        lane8 = lax.broadcasted_iota(I32, cnt.shape, 1)
        incl = padded
        shift = 1
        while shift < LANES:
            incl = incl + jnp.where(lane8 >= shift, pltpu.roll(incl, shift, 1), 0.0)
            shift *= 2
        pstart[...] = incl - padded
        carry[...] = jnp.zeros_like(carry)

    @pl.when(ph == 1)
    def _():
        r = lax.broadcasted_iota(I32, (tm, tm), 0)
        c = lax.broadcasted_iota(I32, (tm, tm), 1)
        before = jnp.where(r > c, 1.0, 0.0).astype(BF16)
        pos = (jnp.dot(before, onehot.astype(BF16), preferred_element_type=F32)
               + carry[0:1, :] + pstart[0:1, :])
        d1 = jnp.sum(jnp.where(lane == e1, pos, 0.0), axis=-1, keepdims=True)
        d2 = jnp.sum(jnp.where(lane == e2, pos, 0.0), axis=-1, keepdims=True)
        dest_ref[...] = jnp.where(lane == 0, d1, jnp.where(lane == 1, d2, 0.0)).astype(I32)

    carry[...] = carry[...] + colsum


def _rank(eid, tm=256):
    S = eid.shape[0]
    return pl.pallas_call(
        _rank_kernel,
        grid=(2, S // tm),
        in_specs=[pl.BlockSpec((tm, LANES), lambda p, i: (i, 0))],
        out_specs=[pl.BlockSpec((tm, LANES), lambda p, i: (i * p, 0)),
                   pl.BlockSpec((SUBLANES, LANES), lambda p, i: (0, 0))],
        out_shape=[jax.ShapeDtypeStruct((S, LANES), I32), jax.ShapeDtypeStruct((SUBLANES, LANES), F32)],
        scratch_shapes=[pltpu.VMEM((SUBLANES, LANES), F32), pltpu.VMEM((SUBLANES, LANES), F32)],
        compiler_params=_cparams(("arbitrary", "arbitrary")),
        name="rank",
    )(eid)


def _scatter_kernel(dest_ref, nval_ref, hn_ref, xb_ref, zeros, sem):
    i = pl.program_id(0)
    tm = hn_ref.shape[0]

    @pl.when(i == 0)
    def _():
        zeros[...] = jnp.zeros_like(zeros)

        def zero_block(b):
            return pltpu.make_async_copy(zeros, xb_ref.at[pl.ds(b * MOE_BLOCK, MOE_BLOCK), :], sem)

        def zissue(b, carry):
            @pl.when(nval_ref[b] < MOE_BLOCK)
            def _():
                zero_block(b).start()
            return carry

        def zdrain(b, carry):
            @pl.when(nval_ref[b] < MOE_BLOCK)
            def _():
                zero_block(b).wait()
            return carry

        lax.fori_loop(0, nval_ref.shape[0], zissue, 0)
        lax.fori_loop(0, nval_ref.shape[0], zdrain, 0)

    def issue(r, carry):
        for k in range(TOP_K):
            d = dest_ref[(i * tm + r) * TOP_K + k]
            pltpu.make_async_copy(hn_ref.at[pl.ds(r, 1), :], xb_ref.at[pl.ds(d, 1), :], sem).start()
        return carry

    lax.fori_loop(0, tm, issue, 0)

    def drain(r, carry):
        for k in range(TOP_K):
            pltpu.make_async_copy(hn_ref.at[pl.ds(0, 1), :], xb_ref.at[pl.ds(0, 1), :], sem).wait()
        return carry

    lax.fori_loop(0, tm, drain, 0)


def _scatter_rows(dest_flat, block_valid, hn, n_rows, tm=256):
    S, D = hn.shape
    return pl.pallas_call(
        _scatter_kernel,
        grid_spec=pltpu.PrefetchScalarGridSpec(
            num_scalar_prefetch=2,
            grid=(S // tm,),
            in_specs=[pl.BlockSpec((tm, D), lambda i, d, nv: (i, 0))],
            out_specs=pl.BlockSpec(memory_space=pl.ANY),
            scratch_shapes=[pltpu.VMEM((MOE_BLOCK, D), F32), pltpu.SemaphoreType.DMA(())]),
        out_shape=jax.ShapeDtypeStruct((n_rows, D), F32),
        compiler_params=_cparams(("arbitrary",)),
        name="scatter",
    )(dest_flat, block_valid, hn)


def _expert_kernel(bexp_ref, nval_ref, x_ref, wg_ref, wu_ref, wd_ref, y_ref, wg_s, wu_s, wd_s):
    b = pl.program_id(0)
    e = bexp_ref[b]
    e_prev = bexp_ref[jnp.maximum(b - 1, 0)]
    nval = nval_ref[b]

    @pl.when(jnp.logical_and(nval > 0, jnp.logical_or(b == 0, e != e_prev)))
    def _():
        wg_s[...] = wg_ref[0].astype(BF16)
        wu_s[...] = wu_ref[0].astype(BF16)
        wd_s[...] = wd_ref[0].astype(BF16)

    @pl.when(nval == 0)
    def _():
        y_ref[...] = jnp.zeros_like(y_ref)

    @pl.when(nval > 0)
    def _():
        x = x_ref[...].astype(BF16)
        gate = jnp.dot(x, wg_s[...], preferred_element_type=F32)
        up = jnp.dot(x, wu_s[...], preferred_element_type=F32)
        hid = gate * jax.nn.sigmoid(gate) * up
        y_ref[...] = jnp.dot(hid.astype(BF16), wd_s[...], preferred_element_type=F32)


def _experts(block_expert, block_valid, xb, w_gate, w_up, w_down):
    P, D = xb.shape
    DE = w_gate.shape[-1]
    nb = P // MOE_BLOCK
    return pl.pallas_call(
        _expert_kernel,
        grid_spec=pltpu.PrefetchScalarGridSpec(
            num_scalar_prefetch=2,
            grid=(nb,),
            in_specs=[pl.BlockSpec((MOE_BLOCK, D), lambda b, be, nv: (b, 0)),
                      pl.BlockSpec((1, D, DE), lambda b, be, nv: (be[b], 0, 0)),
                      pl.BlockSpec((1, D, DE), lambda b, be, nv: (be[b], 0, 0)),
                      pl.BlockSpec((1, DE, D), lambda b, be, nv: (be[b], 0, 0))],
            out_specs=pl.BlockSpec((MOE_BLOCK, D), lambda b, be, nv: (b, 0)),
            scratch_shapes=[pltpu.VMEM((D, DE), BF16), pltpu.VMEM((D, DE), BF16), pltpu.VMEM((DE, D), BF16)]),
        out_shape=jax.ShapeDtypeStruct((P, D), F32),
        compiler_params=_cparams(("arbitrary",)),
        name="experts",
    )(block_expert, block_valid, xb, w_gate, w_up, w_down)


def _combine_kernel(dest_ref, h2_ref, wts_ref, g_ref, yb_ref, o_ref, buf, sem):
    i = pl.program_id(0)
    tm = h2_ref.shape[0]

    def issue(r, carry):
        for k in range(TOP_K):
            d = dest_ref[(i * tm + r) * TOP_K + k]
            pltpu.make_async_copy(yb_ref.at[pl.ds(d, 1), :], buf.at[k, pl.ds(r, 1), :], sem).start()
        return carry

    lax.fori_loop(0, tm, issue, 0)

    def drain(r, carry):
        for k in range(TOP_K):
            pltpu.make_async_copy(yb_ref.at[pl.ds(0, 1), :], buf.at[0, pl.ds(0, 1), :], sem).wait()
        return carry

    lax.fori_loop(0, tm, drain, 0)
    wts = wts_ref[...]
    h = h2_ref[...] + wts[:, 0:1] * buf[0] + wts[:, 1:2] * buf[1]
    o_ref[...] = _rms(h, g_ref[...])


def _combine(dest_flat, h2, wts, g_final, yb, tm=256):
    S, D = h2.shape
    return pl.pallas_call(
        _combine_kernel,
        grid_spec=pltpu.PrefetchScalarGridSpec(
            num_scalar_prefetch=1,
            grid=(S // tm,),
            in_specs=[pl.BlockSpec((tm, D), lambda i, d: (i, 0)),
                      pl.BlockSpec((tm, LANES), lambda i, d: (i, 0)),
                      pl.BlockSpec((1, D), lambda i, d: (0, 0)),
                      pl.BlockSpec(memory_space=pl.ANY)],
            out_specs=pl.BlockSpec((tm, D), lambda i, d: (i, 0)),
            scratch_shapes=[pltpu.VMEM((TOP_K, tm, D), F32), pltpu.SemaphoreType.DMA(())]),
        out_shape=jax.ShapeDtypeStruct((S, D), F32),
        compiler_params=_cparams(("arbitrary",)),
        name="combine",
    )(dest_flat, h2, wts, g_final, yb)


def _lane_pad(v, offset):
    return jnp.zeros((1, LANES), F32).at[0, offset:offset + v.shape[0]].set(v.astype(F32))


def _layer(h, mem, g_mix, w_in, conv_w, a_log, dt_bias, g_delta_out, g_attn_out, w_out, g_cross, g_mem,
           w_q_mem, w_kv_mem, w_o_mem, g_moe, w_group, b_group, w_expert, b_expert, w_gate, w_up, w_down,
           g_final):
    S, D = h.shape
    H = N_HEADS_B
    w_main = w_in[:, :W_MAIN].astype(BF16)
    w_small = jnp.pad(w_in[:, W_MAIN:], ((0, 0), (0, LANES - 2 * H)))
    proj, ba = _inproj(h, g_mix[None], w_main, w_small)

    slopes = jnp.asarray(2.0 ** (-8.0 * np.arange(1, N_HEADS_A + 1) / N_HEADS_A), dtype=F32)
    attn = _dilated_attention(proj, slopes)
    delta = _gated_deltanet(proj, ba, conv_w, _lane_pad(a_log, H), _lane_pad(dt_bias, H), g_delta_out[None])

    k_mem, v_mem = _memkv(mem, g_mem[None], w_kv_mem.astype(BF16))
    w_router = jnp.pad(jnp.concatenate([w_group, w_expert], axis=1), ((0, 0), (0, LANES - N_GROUPS - N_EXPERTS)))
    b_router = _lane_pad(jnp.concatenate([b_group, b_expert]), 0)
    h2, hn, eid, wts = _mix(attn, delta, h, g_attn_out[None], w_out.astype(BF16), g_cross[None],
                            w_q_mem.astype(BF16), k_mem, v_mem, w_o_mem.astype(BF16), g_moe[None],
                            w_router, b_router)

    dest, cnt = _rank(eid)
    dest_flat = dest[:, :TOP_K].reshape(S * TOP_K)
    counts = cnt[0, :N_EXPERTS].astype(I32)
    padded = ((counts + MOE_BLOCK - 1) // MOE_BLOCK) * MOE_BLOCK
    pend = jnp.cumsum(padded)
    n_rows = S * TOP_K + N_EXPERTS * MOE_BLOCK
    block_row = jnp.arange(n_rows // MOE_BLOCK, dtype=I32) * MOE_BLOCK
    block_expert = jnp.minimum(jnp.searchsorted(pend, block_row, side='right'), N_EXPERTS - 1).astype(I32)
    seg_end = (pend - padded + counts)[block_expert]
    block_valid = jnp.clip(seg_end - block_row, 0, MOE_BLOCK).astype(I32)

    xb = _scatter_rows(dest_flat, block_valid, hn, n_rows)
    yb = _experts(block_expert, block_valid, xb, w_gate, w_up, w_down)
    return _combine(dest_flat, h2, wts, g_final[None], yb)


def kernel(x, mem, g_mix, w_in, conv_w, a_log, dt_bias, g_delta_out, g_attn_out, w_out, g_cross, g_mem, w_q_mem,
           w_kv_mem, w_o_mem, g_moe, w_group, b_group, w_expert, b_expert, w_gate, w_up, w_down, g_final):
    assert x.shape[0] == 1 and mem.shape[0] == 1 and g_mix.shape[0] == 1
    out = _layer(x[0].astype(F32), mem[0].astype(F32), g_mix[0], w_in[0], conv_w[0], a_log[0], dt_bias[0],
                 g_delta_out[0], g_attn_out[0], w_out[0], g_cross[0], g_mem[0], w_q_mem[0], w_kv_mem[0],
                 w_o_mem[0], g_moe[0], w_group[0], b_group[0], w_expert[0], b_expert[0], w_gate[0], w_up[0],
                 w_down[0], g_final)
    return out[None].astype(x.dtype)
```

```python
import functools

import numpy as np
import jax
import jax.numpy as jnp
from jax import lax
from jax.experimental import pallas as pl
from jax.experimental.pallas import tpu as pltpu

F32 = jnp.float32
BF16 = jnp.bfloat16
I32 = jnp.int32
HIGHEST = lax.Precision.HIGHEST

EPS = 1e-6
HEAD_DIM = 128
LANES = 128
SUBLANES = 8
NEG = -1e30

N_HEADS_A = 8
N_HEADS_B = 8
W_A = N_HEADS_A * HEAD_DIM
W_B = N_HEADS_B * HEAD_DIM
W_MAIN = 3 * W_A + 4 * W_B
DILATED_PATTERNS = ((128, 1), (512, 4), (2048, 16))
ATTN_BLOCK = 128
ATTN_TILE = 2048
CONV_WIDTH = 4
DELTA_CHUNK = 64
DELTA_BLOCK = 256
N_MEM_HEADS = 4
W_MEM = N_MEM_HEADS * HEAD_DIM
N_GROUPS = 8
EXPERTS_PER_GROUP = 8
N_EXPERTS = 64
TOP_K = 2
MOE_BLOCK = 128

VMEM_LIMIT = 56 * 1024 * 1024


def _cparams(sem):
    return pltpu.CompilerParams(dimension_semantics=sem, vmem_limit_bytes=VMEM_LIMIT)


def _rms(x, g):
    return x * lax.rsqrt(jnp.mean(x * x, axis=-1, keepdims=True) + EPS) * g


def _dot(a, b):
    return jnp.dot(a.astype(BF16), b.astype(BF16), preferred_element_type=F32)


def _dot_nt(a, b):
    return lax.dot_general(a.astype(BF16), b.astype(BF16), (((1,), (1,)), ((), ())), preferred_element_type=F32)


def _dot_tn(a, b):
    return lax.dot_general(a.astype(BF16), b.astype(BF16), (((0,), (0,)), ((), ())), preferred_element_type=F32)


def _split_hi_lo(w):
    hi = w.astype(BF16)
    lo = (w - hi.astype(F32)).astype(BF16)
    return jnp.concatenate([hi, lo], axis=1)


def _dot_x3(x, w_hl_ref):
    x_hi = x.astype(BF16)
    x_lo = (x - x_hi.astype(F32)).astype(BF16)
    r = jnp.dot(x_hi, w_hl_ref[...], preferred_element_type=F32)
    return r[:, :LANES] + r[:, LANES:] + jnp.dot(x_lo, w_hl_ref[:, :LANES], preferred_element_type=F32)


def _inproj_kernel(x_ref, g_ref, w_ref, ws_ref, o_ref, os_ref, u_scr):
    @pl.when(pl.program_id(1) == 0)
    def _():
        u = _rms(x_ref[...], g_ref[...])
        u_scr[...] = u.astype(BF16)
        os_ref[...] = _dot_x3(u, ws_ref)

    o_ref[...] = jnp.dot(u_scr[...], w_ref[...], preferred_element_type=F32)


def _inproj(x, g, w_main, w_small, tm=1024, tn=1024):
    S, D = x.shape
    N = w_main.shape[1]
    return pl.pallas_call(
        _inproj_kernel,
        grid=(S // tm, N // tn),
        in_specs=[pl.BlockSpec((tm, D), lambda i, j: (i, 0)),
                  pl.BlockSpec((1, D), lambda i, j: (0, 0)),
                  pl.BlockSpec((D, tn), lambda i, j: (0, j)),
                  pl.BlockSpec((D, 2 * LANES), lambda i, j: (0, 0))],
        out_specs=[pl.BlockSpec((tm, tn), lambda i, j: (i, j)),
                   pl.BlockSpec((tm, LANES), lambda i, j: (i, 0))],
        out_shape=[jax.ShapeDtypeStruct((S, N), F32), jax.ShapeDtypeStruct((S, LANES), F32)],
        scratch_shapes=[pltpu.VMEM((tm, D), BF16)],
        compiler_params=_cparams(("parallel", "arbitrary")),
        name="inproj",
    )(x, g, w_main, w_small)


def _attn_kernel(slope_ref, q_ref, kp_ref, kc_ref, vp_ref, vc_ref, o_ref, kk, vv, o_scr, l_scr):
    i = pl.program_id(0)
    h = pl.program_id(1)
    T = ATTN_TILE
    B = ATTN_BLOCK
    slope = slope_ref[h]
    kk[0:T, :] = kp_ref[...]
    kk[T:2 * T, :] = kc_ref[...]
    vv[0:T, :] = vp_ref[...]
    vv[T:2 * T, :] = vc_ref[...]
    qi = lax.broadcasted_iota(I32, (B, 2 * B), 0)
    ki = lax.broadcasted_iota(I32, (B, 2 * B), 1)
    dist = qi + B - ki
    band = (dist >= 0) & (dist <= B)
    band_first = band & (ki >= B)
    distf = dist.astype(F32)
    scale = HEAD_DIM ** -0.5

    for p, (window, d) in enumerate(DILATED_PATTERNS):
        assert window // d == B and T % (B * d) == 0
        bias = jnp.where(band, distf * (-slope * d), NEG)
        bias_first = jnp.where(band_first, distf * (-slope * d), NEG)

        def body(b, carry, p=p, d=d, bias=bias, bias_first=bias_first):
            r = b % d
            j = b // d
            qs = r + B * d * j
            ks = T - B * d + qs
            q = q_ref[pl.ds(qs, B, stride=d), :] * scale
            k = kk[pl.ds(ks, 2 * B, stride=d), :]
            v = vv[pl.ds(ks, 2 * B, stride=d), :]
            s = _dot_nt(q, k)
            first = jnp.logical_and(i == 0, j == 0)
            s = s + jnp.where(first, bias_first, bias)
            m = jnp.max(s, axis=-1, keepdims=True)
            e = jnp.exp(s - m)
            l = jnp.sum(e, axis=-1, keepdims=True)
            o = _dot(e, v) / l
            lse = m + jnp.log(l)
            o_scr[p, pl.ds(qs, B, stride=d), :] = o
            l_scr[p, pl.ds(qs, B, stride=d), :] = jnp.broadcast_to(lse, (B, HEAD_DIM))
            return carry

        lax.fori_loop(0, T // B, body, 0, unroll=8)

    l0, l1, l2 = l_scr[0], l_scr[1], l_scr[2]
    m = jnp.maximum(jnp.maximum(l0, l1), l2)
    w0, w1, w2 = jnp.exp(l0 - m), jnp.exp(l1 - m), jnp.exp(l2 - m)
    o_ref[...] = (w0 * o_scr[0] + w1 * o_scr[1] + w2 * o_scr[2]) / (w0 + w1 + w2)


def _dilated_attention(proj, slopes):
    S = proj.shape[0]
    T = ATTN_TILE
    H = N_HEADS_A
    blk = (T, HEAD_DIM)
    return pl.pallas_call(
        _attn_kernel,
        grid_spec=pltpu.PrefetchScalarGridSpec(
            num_scalar_prefetch=1,
            grid=(S // T, H),
            in_specs=[pl.BlockSpec(blk, lambda i, h, s: (i, h)),
                      pl.BlockSpec(blk, lambda i, h, s: (jnp.maximum(i - 1, 0), H + h)),
                      pl.BlockSpec(blk, lambda i, h, s: (i, H + h)),
                      pl.BlockSpec(blk, lambda i, h, s: (jnp.maximum(i - 1, 0), 2 * H + h)),
                      pl.BlockSpec(blk, lambda i, h, s: (i, 2 * H + h))],
            out_specs=pl.BlockSpec(blk, lambda i, h, s: (i, h)),
            scratch_shapes=[pltpu.VMEM((2 * T, HEAD_DIM), F32), pltpu.VMEM((2 * T, HEAD_DIM), F32),
                            pltpu.VMEM((3, T, HEAD_DIM), F32), pltpu.VMEM((3, T, HEAD_DIM), F32)]),
        out_shape=jax.ShapeDtypeStruct((S, W_A), F32),
        compiler_params=_cparams(("parallel", "parallel")),
        name="attn",
    )(slopes, proj, proj, proj, proj, proj)


def _softplus(x):
    return jnp.maximum(x, 0.0) + jnp.log1p(jnp.exp(-jnp.abs(x)))


def _delta_kernel(hist_ref, qkv_ref, z_ref, ba_ref, cw_ref, alog_ref, dtb_ref, gout_ref, o_ref,
                  ext, act, gct, state):
    i = pl.program_id(0)
    CB = DELTA_BLOCK
    C = DELTA_CHUNK
    H = N_HEADS_B
    D = HEAD_DIM

    @pl.when(i == 0)
    def _():
        state[...] = jnp.zeros_like(state)
        ext[0:SUBLANES, :] = jnp.zeros((SUBLANES, 3 * W_B), F32)

    @pl.when(i > 0)
    def _():
        ext[0:SUBLANES, :] = hist_ref[...]

    ext[SUBLANES:SUBLANES + CB, :] = qkv_ref[...]

    for c in range(3 * H):
        sl = slice(c * D, (c + 1) * D)
        acc = cw_ref[CONV_WIDTH - 1:CONV_WIDTH, sl] * ext[SUBLANES:SUBLANES + CB, sl]
        for t in range(1, CONV_WIDTH):
            acc = acc + cw_ref[CONV_WIDTH - 1 - t:CONV_WIDTH - t, sl] * ext[SUBLANES - t:SUBLANES - t + CB, sl]
        a = acc * jax.nn.sigmoid(acc)
        if c < 2 * H:
            a = a * lax.rsqrt(jnp.sum(a * a, axis=-1, keepdims=True) + EPS)
        if c < H:
            a = a * (D ** -0.5)
        act[:, sl] = a

    ba = ba_ref[...]
    beta = jax.nn.sigmoid(ba)
    g = -jnp.exp(alog_ref[...]) * _softplus(ba + dtb_ref[...])
    row = lax.broadcasted_iota(I32, (CB, CB), 0)
    col = lax.broadcasted_iota(I32, (CB, CB), 1)
    chunk_tri = jnp.where((row // C == col // C) & (row >= col), 1.0, 0.0).astype(F32)
    gc = jnp.dot(chunk_tri, g, precision=HIGHEST, preferred_element_type=F32)
    gct[...] = gc.T
    eg = jnp.exp(gc)

    ri = lax.broadcasted_iota(I32, (C, C), 0)
    ci = lax.broadcasted_iota(I32, (C, C), 1)
    causal = ri >= ci
    strict = ri > ci
    gout = gout_ref[...]

    low = lax.broadcasted_iota(I32, (C, 2 * C), 1) < C
    heads = range(H)
    for n in range(CB // C):
        rows = slice(n * C, (n + 1) * C)
        q = [act[rows, h * D:(h + 1) * D] for h in heads]
        k = [act[rows, (H + h) * D:(H + h + 1) * D] for h in heads]
        v = [act[rows, (2 * H + h) * D:(2 * H + h + 1) * D] for h in heads]
        beta_c = [beta[rows, h:h + 1] for h in heads]
        gc_c = [gc[rows, H + h:H + h + 1] for h in heads]
        eg_c = [eg[rows, H + h:H + h + 1] for h in heads]
        gc_r = [gct[H + h:H + h + 1, rows] for h in heads]
        g_last = [gc_r[h][:, C - 1:C] for h in heads]
        decay = [jnp.exp(jnp.where(causal, gc_c[h] - gc_r[h], NEG)) for h in heads]
        kb = [k[h] * beta_c[h] for h in heads]
        kq = [_dot_nt(jnp.concatenate([kb[h], q[h]], axis=0), k[h]) for h in heads]
        qk = [jnp.where(causal, kq[h][C:] * decay[h], 0.0) for h in heads]
        m = [jnp.concatenate([jnp.where(strict, -kq[h][:C] * decay[h], 0.0), jnp.zeros((C, C), F32)], axis=1)
             for h in heads]
        for _ in range(int(np.log2(C))):
            pm = [_dot(m[h][:, :C], m[h]) for h in heads]
            m = [jnp.where(low, pm[h], m[h] + pltpu.roll(m[h], C, 1) + pm[h]) for h in heads]
        bmat = [jnp.concatenate([v[h] * beta_c[h], kb[h] * eg_c[h]], axis=-1) for h in heads]
        uw = [bmat[h] + _dot(m[h][:, C:], bmat[h]) for h in heads]
        wq = [jnp.concatenate([uw[h][:, D:], q[h] * eg_c[h]], axis=0) for h in heads]
        kd = [k[h] * jnp.exp(g_last[h] - gc_c[h]) for h in heads]
        s_old = [state[h] for h in heads]
        ws = [_dot(wq[h], s_old[h]) for h in heads]
        v_new = [uw[h][:, :D] - ws[h][:C] for h in heads]
        for h in heads:
            state[h] = s_old[h] * jnp.exp(g_last[h]) + _dot_tn(kd[h], v_new[h])
        o = [ws[h][C:] + _dot(qk[h], v_new[h]) for h in heads]
        for h in heads:
            z = z_ref[rows, h * D:(h + 1) * D]
            o_ref[rows, h * D:(h + 1) * D] = _rms(o[h], gout) * (z * jax.nn.sigmoid(z))


def _gated_deltanet(proj, ba, conv_w, alog_pad, dtb_pad, g_out):
    S = proj.shape[0]
    CB = DELTA_BLOCK
    qkv_col = W_A * 3 // (3 * W_B)
    z_col = (3 * W_A + 3 * W_B) // W_B
    return pl.pallas_call(
        _delta_kernel,
        grid=(S // CB,),
        in_specs=[pl.BlockSpec((SUBLANES, 3 * W_B), lambda i: (jnp.maximum(i * (CB // SUBLANES) - 1, 0), qkv_col)),
                  pl.BlockSpec((CB, 3 * W_B), lambda i: (i, qkv_col)),
                  pl.BlockSpec((CB, W_B), lambda i: (i, z_col)),
                  pl.BlockSpec((CB, LANES), lambda i: (i, 0)),
                  pl.BlockSpec((CONV_WIDTH, 3 * W_B), lambda i: (0, 0)),
                  pl.BlockSpec((1, LANES), lambda i: (0, 0)),
                  pl.BlockSpec((1, LANES), lambda i: (0, 0)),
                  pl.BlockSpec((1, HEAD_DIM), lambda i: (0, 0))],
        out_specs=pl.BlockSpec((CB, W_B), lambda i: (i, 0)),
        out_shape=jax.ShapeDtypeStruct((S, W_B), F32),
        scratch_shapes=[pltpu.VMEM((CB + SUBLANES, 3 * W_B), F32),
                        pltpu.VMEM((CB, 3 * W_B), F32),
                        pltpu.VMEM((LANES, CB), F32),
                        pltpu.VMEM((N_HEADS_B, HEAD_DIM, HEAD_DIM), F32)],
        compiler_params=_cparams(("arbitrary",)),
        name="delta",
    )(proj, proj, proj, ba, conv_w, alog_pad, dtb_pad, g_out)


def _memkv_kernel(mem_ref, g_ref, w_ref, k_ref, v_ref):
    kv = _dot(_rms(mem_ref[...], g_ref[...]), w_ref[...])
    k_ref[...] = kv[:, :W_MEM].astype(BF16)
    v_ref[...] = kv[:, W_MEM:].astype(BF16)


def _memkv(mem, g, w_kv):
    M = mem.shape[0]
    return pl.pallas_call(
        _memkv_kernel,
        out_shape=[jax.ShapeDtypeStruct((M, W_MEM), BF16), jax.ShapeDtypeStruct((M, W_MEM), BF16)],
        compiler_params=pltpu.CompilerParams(vmem_limit_bytes=VMEM_LIMIT),
        name="memkv",
    )(mem, g, w_kv)


def _mix_kernel(attn_ref, delta_ref, x_ref, gattn_ref, wout_ref, gcross_ref, wq_ref, km_ref, vm_ref, wo_ref,
                gmoe_ref, wr_ref, br_ref, h2_ref, hn_ref, eid_ref, wts_ref):
    an = _rms(attn_ref[...], gattn_ref[...])
    mix = jnp.concatenate([an.astype(BF16), delta_ref[...].astype(BF16)], axis=-1)
    h1 = x_ref[...] + jnp.dot(mix, wout_ref[...], preferred_element_type=F32)

    q = _dot(_rms(h1, gcross_ref[...]), wq_ref[...]) * (HEAD_DIM ** -0.5)
    outs = []
    for hh in range(N_MEM_HEADS):
        sl = slice(hh * HEAD_DIM, (hh + 1) * HEAD_DIM)
        s = _dot_nt(q[:, sl], km_ref[:, sl])
        e = jnp.exp(s - jnp.max(s, axis=-1, keepdims=True))
        outs.append(_dot(e, vm_ref[:, sl]) / jnp.sum(e, axis=-1, keepdims=True))
    h2 = h1 + _dot(jnp.concatenate(outs, axis=-1), wo_ref[...])
    h2_ref[...] = h2

    hn = _rms(h2, gmoe_ref[...])
    hn_ref[...] = hn
    logits = _dot_x3(hn, wr_ref) + br_ref[...]
    lane = lax.broadcasted_iota(I32, logits.shape, 1)
    gl = jnp.where(lane < N_GROUPS, logits, NEG)
    gmax = jnp.max(gl, axis=-1, keepdims=True)
    g_sel = jnp.min(jnp.where(gl == gmax, lane, LANES), axis=-1, keepdims=True)
    g_gate = 1.0 / jnp.sum(jnp.exp(gl - gmax), axis=-1, keepdims=True)
    in_group = (lane >= N_GROUPS) & ((lane - N_GROUPS) // EXPERTS_PER_GROUP == g_sel)
    el = jnp.where(in_group, logits, NEG)
    v1 = jnp.max(el, axis=-1, keepdims=True)
    i1 = jnp.min(jnp.where(in_group & (el == v1), lane, LANES), axis=-1, keepdims=True)
    in_rest = in_group & (lane != i1)
    el2 = jnp.where(in_rest, logits, NEG)
    v2 = jnp.max(el2, axis=-1, keepdims=True)
    i2 = jnp.min(jnp.where(in_rest & (el2 == v2), lane, LANES), axis=-1, keepdims=True)
    e2 = jnp.exp(v2 - v1)
    w1 = g_gate / (1.0 + e2)
    w2 = g_gate * e2 / (1.0 + e2)
    eid_ref[...] = jnp.where(lane == 0, i1 - N_GROUPS, jnp.where(lane == 1, i2 - N_GROUPS, 0))
    wts_ref[...] = jnp.where(lane == 0, w1, jnp.where(lane == 1, w2, 0.0))


def _mix(attn, delta, x, g_attn, w_out, g_cross, w_q, k_mem, v_mem, w_o, g_moe, w_router, b_router, tm=256):
    S, D = x.shape
    M = k_mem.shape[0]
    row = lambda w: pl.BlockSpec((tm, w), lambda i: (i, 0))
    full = lambda a: pl.BlockSpec(a.shape, lambda i: (0, 0))
    return pl.pallas_call(
        _mix_kernel,
        grid=(S // tm,),
        in_specs=[row(W_A), row(W_B), row(D), full(g_attn), full(w_out), full(g_cross), full(w_q),
                  full(k_mem), full(v_mem), full(w_o), full(g_moe), full(w_router), full(b_router)],
        out_specs=[row(D), row(D), row(LANES), row(LANES)],
        out_shape=[jax.ShapeDtypeStruct((S, D), F32), jax.ShapeDtypeStruct((S, D), F32),
                   jax.ShapeDtypeStruct((S, LANES), I32), jax.ShapeDtypeStruct((S, LANES), F32)],
        compiler_params=_cparams(("parallel",)),
        name="mix",
    )(attn, delta, x, g_attn, w_out, g_cross, w_q, k_mem, v_mem, w_o, g_moe, w_router, b_router)


def _rank_kernel(eid_ref, dest_ref, cnt_ref, carry, pstart):
    ph = pl.program_id(0)
    i = pl.program_id(1)
    tm = eid_ref.shape[0]
    lane = lax.broadcasted_iota(I32, (tm, LANES), 1)
    e = eid_ref[...]
    e1 = e[:, 0:1]
    e2 = e[:, 1:2]
    onehot = jnp.where((lane == e1) | (lane == e2), 1.0, 0.0).astype(F32)
    colsum = jnp.sum(onehot, axis=0, keepdims=True)

    @pl.when(jnp.logical_and(ph == 0, i == 0))
    def _():
        carry[...] = jnp.zeros_like(carry)

    @pl.when(jnp.logical_and(ph == 1, i == 0))
    def _():
        cnt = carry[...]
        cnt_ref[...] = cnt
        padded = jnp.floor((cnt + (MOE_BLOCK - 1)) * (1.0 / MOE_BLOCK)) * MOE_BLOCK
        lane8 = lax.broadcasted_iota(I32, cnt.shape, 1)
        incl = padded
        shift = 1
        while shift < LANES:
            incl = incl + jnp.where(lane8 >= shift, pltpu.roll(incl, shift, 1), 0.0)
            shift *= 2
        pstart[...] = incl - padded
        carry[...] = jnp.zeros_like(carry)

    @pl.when(ph == 1)
    def _():
        r = lax.broadcasted_iota(I32, (tm, tm), 0)
        c = lax.broadcasted_iota(I32, (tm, tm), 1)
        before = jnp.where(r > c, 1.0, 0.0).astype(BF16)
        pos = (jnp.dot(before, onehot.astype(BF16), preferred_element_type=F32)
               + carry[0:1, :] + pstart[0:1, :])
        d1 = jnp.sum(jnp.where(lane == e1, pos, 0.0), axis=-1, keepdims=True)
        d2 = jnp.sum(jnp.where(lane == e2, pos, 0.0), axis=-1, keepdims=True)
        dest_ref[...] = jnp.where(lane == 0, d1, jnp.where(lane == 1, d2, 0.0)).astype(I32)

    carry[...] = carry[...] + colsum


def _rank(eid, tm=256):
    S = eid.shape[0]
    return pl.pallas_call(
        _rank_kernel,
        grid=(2, S // tm),
        in_specs=[pl.BlockSpec((tm, LANES), lambda p, i: (i, 0))],
        out_specs=[pl.BlockSpec((tm, LANES), lambda p, i: (i * p, 0)),
                   pl.BlockSpec((SUBLANES, LANES), lambda p, i: (0, 0))],
        out_shape=[jax.ShapeDtypeStruct((S, LANES), I32), jax.ShapeDtypeStruct((SUBLANES, LANES), F32)],
        scratch_shapes=[pltpu.VMEM((SUBLANES, LANES), F32), pltpu.VMEM((SUBLANES, LANES), F32)],
        compiler_params=_cparams(("arbitrary", "arbitrary")),
        name="rank",
    )(eid)


def _scatter_kernel(dest_ref, nval_ref, hn_ref, xb_ref, zeros, sem):
    i = pl.program_id(0)
    tm = hn_ref.shape[0]

    @pl.when(i == 0)
    def _():
        zeros[...] = jnp.zeros_like(zeros)

        def zero_block(b):
            return pltpu.make_async_copy(zeros, xb_ref.at[pl.ds(b * MOE_BLOCK, MOE_BLOCK), :], sem)

        def zissue(b, carry):
            @pl.when(nval_ref[b] < MOE_BLOCK)
            def _():
                zero_block(b).start()
            return carry

        def zdrain(b, carry):
            @pl.when(nval_ref[b] < MOE_BLOCK)
            def _():
                zero_block(b).wait()
            return carry

        lax.fori_loop(0, nval_ref.shape[0], zissue, 0)
        lax.fori_loop(0, nval_ref.shape[0], zdrain, 0)

    def issue(r, carry):
        for k in range(TOP_K):
            d = dest_ref[(i * tm + r) * TOP_K + k]
            pltpu.make_async_copy(hn_ref.at[pl.ds(r, 1), :], xb_ref.at[pl.ds(d, 1), :], sem).start()
        return carry

    lax.fori_loop(0, tm, issue, 0)

    def drain(r, carry):
        for k in range(TOP_K):
            pltpu.make_async_copy(hn_ref.at[pl.ds(0, 1), :], xb_ref.at[pl.ds(0, 1), :], sem).wait()
        return carry

    lax.fori_loop(0, tm, drain, 0)


def _scatter_rows(dest_flat, block_valid, hn, n_rows, tm=256):
    S, D = hn.shape
    return pl.pallas_call(
        _scatter_kernel,
        grid_spec=pltpu.PrefetchScalarGridSpec(
            num_scalar_prefetch=2,
            grid=(S // tm,),
            in_specs=[pl.BlockSpec((tm, D), lambda i, d, nv: (i, 0))],
            out_specs=pl.BlockSpec(memory_space=pl.ANY),
            scratch_shapes=[pltpu.VMEM((MOE_BLOCK, D), F32), pltpu.SemaphoreType.DMA(())]),
        out_shape=jax.ShapeDtypeStruct((n_rows, D), F32),
        compiler_params=_cparams(("arbitrary",)),
        name="scatter",
    )(dest_flat, block_valid, hn)


def _expert_kernel(bexp_ref, nval_ref, slot_ref, next_ref, x_ref, wg_ref, wu_ref, wd_ref, y_ref,
                   wg_f, wu_f, wd_f, wg_s, wu_s, wd_s, sem):
    b = pl.program_id(0)
    e = bexp_ref[b]
    e_prev = bexp_ref[jnp.maximum(b - 1, 0)]
    nval = nval_ref[b]

    def weight_copies(ex, slot):
        return (pltpu.make_async_copy(wg_ref.at[ex], wg_f.at[slot], sem.at[slot, 0]),
                pltpu.make_async_copy(wu_ref.at[ex], wu_f.at[slot], sem.at[slot, 1]),
                pltpu.make_async_copy(wd_ref.at[ex], wd_f.at[slot], sem.at[slot, 2]))

    @pl.when(jnp.logical_and(b == 0, nval > 0))
    def _():
        for c in weight_copies(e, slot_ref[e]):
            c.start()

    @pl.when(jnp.logical_and(nval > 0, jnp.logical_or(b == 0, e != e_prev)))
    def _():
        slot = slot_ref[e]
        for c in weight_copies(e, slot):
            c.wait()
        nxt = next_ref[e]

        @pl.when(nxt >= 0)
        def _():
            for c in weight_copies(nxt, 1 - slot):
                c.start()

        wg_s[...] = wg_f[slot].astype(BF16)
        wu_s[...] = wu_f[slot].astype(BF16)
        wd_s[...] = wd_f[slot].astype(BF16)

    @pl.when(nval == 0)
    def _():
        y_ref[...] = jnp.zeros_like(y_ref)

    @pl.when(nval > 0)
    def _():
        x = x_ref[...].astype(BF16)
        gate = jnp.dot(x, wg_s[...], preferred_element_type=F32)
        up = jnp.dot(x, wu_s[...], preferred_element_type=F32)
        hid = gate * jax.nn.sigmoid(gate) * up
        y_ref[...] = jnp.dot(hid.astype(BF16), wd_s[...], preferred_element_type=F32)


def _experts(block_expert, block_valid, expert_slot, expert_next, xb, w_gate, w_up, w_down):
    P, D = xb.shape
    DE = w_gate.shape[-1]
    nb = P // MOE_BLOCK
    row_blk = pl.BlockSpec((MOE_BLOCK, D), lambda b, *_: (b, 0))
    hbm = pl.BlockSpec(memory_space=pl.ANY)
    return pl.pallas_call(
        _expert_kernel,
        grid_spec=pltpu.PrefetchScalarGridSpec(
            num_scalar_prefetch=4,
            grid=(nb,),
            in_specs=[row_blk, hbm, hbm, hbm],
            out_specs=row_blk,
            scratch_shapes=[pltpu.VMEM((2, D, DE), F32), pltpu.VMEM((2, D, DE), F32), pltpu.VMEM((2, DE, D), F32),
                            pltpu.VMEM((D, DE), BF16), pltpu.VMEM((D, DE), BF16), pltpu.VMEM((DE, D), BF16),
                            pltpu.SemaphoreType.DMA((2, 3))]),
        out_shape=jax.ShapeDtypeStruct((P, D), F32),
        compiler_params=_cparams(("arbitrary",)),
        name="experts",
    )(block_expert, block_valid, expert_slot, expert_next, xb, w_gate, w_up, w_down)


def _combine_kernel(dest_ref, h2_ref, wts_ref, g_ref, yb_ref, o_ref, buf, sem):
    i = pl.program_id(0)
    tm = h2_ref.shape[0]

    def issue(r, carry):
        for k in range(TOP_K):
            d = dest_ref[(i * tm + r) * TOP_K + k]
            pltpu.make_async_copy(yb_ref.at[pl.ds(d, 1), :], buf.at[k, pl.ds(r, 1), :], sem).start()
        return carry

    lax.fori_loop(0, tm, issue, 0)

    def drain(r, carry):
        for k in range(TOP_K):
            pltpu.make_async_copy(yb_ref.at[pl.ds(0, 1), :], buf.at[0, pl.ds(0, 1), :], sem).wait()
        return carry

    lax.fori_loop(0, tm, drain, 0)
    wts = wts_ref[...]
    h = h2_ref[...] + wts[:, 0:1] * buf[0] + wts[:, 1:2] * buf[1]
    o_ref[...] = _rms(h, g_ref[...])


def _combine(dest_flat, h2, wts, g_final, yb, tm=256):
    S, D = h2.shape
    return pl.pallas_call(
        _combine_kernel,
        grid_spec=pltpu.PrefetchScalarGridSpec(
            num_scalar_prefetch=1,
            grid=(S // tm,),
            in_specs=[pl.BlockSpec((tm, D), lambda i, d: (i, 0)),
                      pl.BlockSpec((tm, LANES), lambda i, d: (i, 0)),
                      pl.BlockSpec((1, D), lambda i, d: (0, 0)),
                      pl.BlockSpec(memory_space=pl.ANY)],
            out_specs=pl.BlockSpec((tm, D), lambda i, d: (i, 0)),
            scratch_shapes=[pltpu.VMEM((TOP_K, tm, D), F32), pltpu.SemaphoreType.DMA(())]),
        out_shape=jax.ShapeDtypeStruct((S, D), F32),
        compiler_params=_cparams(("arbitrary",)),
        name="combine",
    )(dest_flat, h2, wts, g_final, yb)


def _lane_pad(v, offset):
    return jnp.zeros((1, LANES), F32).at[0, offset:offset + v.shape[0]].set(v.astype(F32))


def _layer(h, mem, g_mix, w_in, conv_w, a_log, dt_bias, g_delta_out, g_attn_out, w_out, g_cross, g_mem,
           w_q_mem, w_kv_mem, w_o_mem, g_moe, w_group, b_group, w_expert, b_expert, w_gate, w_up, w_down,
           g_final):
    S, D = h.shape
    H = N_HEADS_B
    w_main = w_in[:, :W_MAIN].astype(BF16)
    w_small = _split_hi_lo(jnp.pad(w_in[:, W_MAIN:], ((0, 0), (0, LANES - 2 * H))))
    proj, ba = _inproj(h, g_mix[None], w_main, w_small)

    slopes = jnp.asarray(2.0 ** (-8.0 * np.arange(1, N_HEADS_A + 1) / N_HEADS_A), dtype=F32)
    attn = _dilated_attention(proj, slopes)
    delta = _gated_deltanet(proj, ba, conv_w, _lane_pad(a_log, H), _lane_pad(dt_bias, H), g_delta_out[None])

    k_mem, v_mem = _memkv(mem, g_mem[None], w_kv_mem.astype(BF16))
    w_router = _split_hi_lo(jnp.pad(jnp.concatenate([w_group, w_expert], axis=1),
                                    ((0, 0), (0, LANES - N_GROUPS - N_EXPERTS))))
    b_router = _lane_pad(jnp.concatenate([b_group, b_expert]), 0)
    h2, hn, eid, wts = _mix(attn, delta, h, g_attn_out[None], w_out.astype(BF16), g_cross[None],
                            w_q_mem.astype(BF16), k_mem, v_mem, w_o_mem.astype(BF16), g_moe[None],
                            w_router, b_router)

    dest, cnt = _rank(eid)
    dest_flat = dest[:, :TOP_K].reshape(S * TOP_K)
    counts = cnt[0, :N_EXPERTS].astype(I32)
    padded = ((counts + MOE_BLOCK - 1) // MOE_BLOCK) * MOE_BLOCK
    pend = jnp.cumsum(padded)
    n_rows = S * TOP_K + N_EXPERTS * MOE_BLOCK
    block_row = jnp.arange(n_rows // MOE_BLOCK, dtype=I32) * MOE_BLOCK
    block_expert = jnp.minimum(jnp.sum((block_row[:, None] >= pend[None, :]).astype(I32), axis=1), N_EXPERTS - 1)
    seg_end = (pend - padded + counts)[block_expert]
    block_valid = jnp.clip(seg_end - block_row, 0, MOE_BLOCK).astype(I32)

    present = counts > 0
    expert_slot = ((jnp.cumsum(present.astype(I32)) - 1) % 2).astype(I32)
    eidx = jnp.where(present, jnp.arange(N_EXPERTS, dtype=I32), N_EXPERTS)
    at_or_after = lax.cummin(eidx, reverse=True)
    expert_next = jnp.concatenate([at_or_after[1:], jnp.full((1,), N_EXPERTS, I32)])
    expert_next = jnp.where(expert_next < N_EXPERTS, expert_next, -1).astype(I32)

    xb = _scatter_rows(dest_flat, block_valid, hn, n_rows)
    yb = _experts(block_expert, block_valid, expert_slot, expert_next, xb, w_gate, w_up, w_down)
    return _combine(dest_flat, h2, wts, g_final[None], yb)


def kernel(x, mem, g_mix, w_in, conv_w, a_log, dt_bias, g_delta_out, g_attn_out, w_out, g_cross, g_mem, w_q_mem,
           w_kv_mem, w_o_mem, g_moe, w_group, b_group, w_expert, b_expert, w_gate, w_up, w_down, g_final):
    assert x.shape[0] == 1 and mem.shape[0] == 1 and g_mix.shape[0] == 1
    out = _layer(x[0].astype(F32), mem[0].astype(F32), g_mix[0], w_in[0], conv_w[0], a_log[0], dt_bias[0],
                 g_delta_out[0], g_attn_out[0], w_out[0], g_cross[0], g_mem[0], w_q_mem[0], w_kv_mem[0],
                 w_o_mem[0], g_moe[0], w_group[0], b_group[0], w_expert[0], b_expert[0], w_gate[0], w_up[0],
                 w_down[0], g_final)
    return out[None].astype(x.dtype)
```

```python
import functools

import numpy as np
import jax
import jax.numpy as jnp
from jax import lax
from jax.experimental import pallas as pl
from jax.experimental.pallas import tpu as pltpu

F32 = jnp.float32
BF16 = jnp.bfloat16
I32 = jnp.int32
HIGHEST = lax.Precision.HIGHEST

EPS = 1e-6
HEAD_DIM = 128
LANES = 128
SUBLANES = 8
NEG = -1e30

N_HEADS_A = 8
N_HEADS_B = 8
W_A = N_HEADS_A * HEAD_DIM
W_B = N_HEADS_B * HEAD_DIM
W_MAIN = 3 * W_A + 4 * W_B
DILATED_PATTERNS = ((128, 1), (512, 4), (2048, 16))
ATTN_BLOCK = 128
ATTN_TILE = 2048
CONV_WIDTH = 4
DELTA_CHUNK = 64
DELTA_BLOCK = 256
N_MEM_HEADS = 4
W_MEM = N_MEM_HEADS * HEAD_DIM
N_GROUPS = 8
EXPERTS_PER_GROUP = 8
N_EXPERTS = 64
TOP_K = 2
MOE_BLOCK = 128
EXPERT_DMA_CHUNKS = 8
ROW_DMA_UNROLL = 8

VMEM_LIMIT = 56 * 1024 * 1024


def _cparams(sem):
    return pltpu.CompilerParams(dimension_semantics=sem, vmem_limit_bytes=VMEM_LIMIT)


def _rms(x, g):
    return x * lax.rsqrt(jnp.mean(x * x, axis=-1, keepdims=True) + EPS) * g


def _dot(a, b):
    return jnp.dot(a.astype(BF16), b.astype(BF16), preferred_element_type=F32)


def _dot_nt(a, b):
    return lax.dot_general(a.astype(BF16), b.astype(BF16), (((1,), (1,)), ((), ())), preferred_element_type=F32)


def _dot_tn(a, b):
    return lax.dot_general(a.astype(BF16), b.astype(BF16), (((0,), (0,)), ((), ())), preferred_element_type=F32)


def _split_hi_lo(w):
    hi = w.astype(BF16)
    lo = (w - hi.astype(F32)).astype(BF16)
    return jnp.concatenate([hi, lo], axis=1)


def _dot_x3(x, w_hl_ref):
    x_hi = x.astype(BF16)
    x_lo = (x - x_hi.astype(F32)).astype(BF16)
    r = jnp.dot(x_hi, w_hl_ref[...], preferred_element_type=F32)
    return r[:, :LANES] + r[:, LANES:] + jnp.dot(x_lo, w_hl_ref[:, :LANES], preferred_element_type=F32)


def _inproj_kernel(x_ref, g_ref, w_ref, ws_ref, o_ref, os_ref, u_scr):
    @pl.when(pl.program_id(1) == 0)
    def _():
        u = _rms(x_ref[...], g_ref[...])
        u_scr[...] = u.astype(BF16)
        os_ref[...] = _dot_x3(u, ws_ref)

    o_ref[...] = jnp.dot(u_scr[...], w_ref[...], preferred_element_type=F32)


def _inproj(x, g, w_main, w_small, tm=1024, tn=1024):
    S, D = x.shape
    N = w_main.shape[1]
    return pl.pallas_call(
        _inproj_kernel,
        grid=(S // tm, N // tn),
        in_specs=[pl.BlockSpec((tm, D), lambda i, j: (i, 0)),
                  pl.BlockSpec((1, D), lambda i, j: (0, 0)),
                  pl.BlockSpec((D, tn), lambda i, j: (0, j)),
                  pl.BlockSpec((D, 2 * LANES), lambda i, j: (0, 0))],
        out_specs=[pl.BlockSpec((tm, tn), lambda i, j: (i, j)),
                   pl.BlockSpec((tm, LANES), lambda i, j: (i, 0))],
        out_shape=[jax.ShapeDtypeStruct((S, N), F32), jax.ShapeDtypeStruct((S, LANES), F32)],
        scratch_shapes=[pltpu.VMEM((tm, D), BF16)],
        compiler_params=_cparams(("parallel", "arbitrary")),
        name="inproj",
    )(x, g, w_main, w_small)


def _attn_kernel(slope_ref, q_ref, kp_ref, kc_ref, vp_ref, vc_ref, o_ref, kk, vv, o_scr, l_scr):
    i = pl.program_id(0)
    h = pl.program_id(1)
    T = ATTN_TILE
    B = ATTN_BLOCK
    slope = slope_ref[h]
    kk[0:T, :] = kp_ref[...]
    kk[T:2 * T, :] = kc_ref[...]
    vv[0:T, :] = vp_ref[...]
    vv[T:2 * T, :] = vc_ref[...]
    qi = lax.broadcasted_iota(I32, (B, 2 * B), 0)
    ki = lax.broadcasted_iota(I32, (B, 2 * B), 1)
    dist = qi + B - ki
    band = (dist >= 0) & (dist <= B)
    band_first = band & (ki >= B)
    distf = dist.astype(F32)
    scale = HEAD_DIM ** -0.5

    for p, (window, d) in enumerate(DILATED_PATTERNS):
        assert window // d == B and T % (B * d) == 0
        bias = jnp.where(band, distf * (-slope * d), NEG)
        bias_first = jnp.where(band_first, distf * (-slope * d), NEG)

        def body(b, carry, p=p, d=d, bias=bias, bias_first=bias_first):
            r = b % d
            j = b // d
            qs = r + B * d * j
            ks = T - B * d + qs
            q = q_ref[pl.ds(qs, B, stride=d), :] * scale
            k = kk[pl.ds(ks, 2 * B, stride=d), :]
            v = vv[pl.ds(ks, 2 * B, stride=d), :]
            s = _dot_nt(q, k)
            first = jnp.logical_and(i == 0, j == 0)
            s = s + jnp.where(first, bias_first, bias)
            m = jnp.max(s, axis=-1, keepdims=True)
            e = jnp.exp(s - m)
            l = jnp.sum(e, axis=-1, keepdims=True)
            o = _dot(e, v) / l
            lse = m + jnp.log(l)
            o_scr[p, pl.ds(qs, B, stride=d), :] = o
            l_scr[p, pl.ds(qs, B, stride=d), :] = jnp.broadcast_to(lse, (B, HEAD_DIM))
            return carry

        lax.fori_loop(0, T // B, body, 0, unroll=8)

    l0, l1, l2 = l_scr[0], l_scr[1], l_scr[2]
    m = jnp.maximum(jnp.maximum(l0, l1), l2)
    w0, w1, w2 = jnp.exp(l0 - m), jnp.exp(l1 - m), jnp.exp(l2 - m)
    o_ref[...] = (w0 * o_scr[0] + w1 * o_scr[1] + w2 * o_scr[2]) / (w0 + w1 + w2)


def _dilated_attention(proj, slopes):
    S = proj.shape[0]
    T = ATTN_TILE
    H = N_HEADS_A
    blk = (T, HEAD_DIM)
    return pl.pallas_call(
        _attn_kernel,
        grid_spec=pltpu.PrefetchScalarGridSpec(
            num_scalar_prefetch=1,
            grid=(S // T, H),
            in_specs=[pl.BlockSpec(blk, lambda i, h, s: (i, h)),
                      pl.BlockSpec(blk, lambda i, h, s: (jnp.maximum(i - 1, 0), H + h)),
                      pl.BlockSpec(blk, lambda i, h, s: (i, H + h)),
                      pl.BlockSpec(blk, lambda i, h, s: (jnp.maximum(i - 1, 0), 2 * H + h)),
                      pl.BlockSpec(blk, lambda i, h, s: (i, 2 * H + h))],
            out_specs=pl.BlockSpec(blk, lambda i, h, s: (i, h)),
            scratch_shapes=[pltpu.VMEM((2 * T, HEAD_DIM), F32), pltpu.VMEM((2 * T, HEAD_DIM), F32),
                            pltpu.VMEM((3, T, HEAD_DIM), F32), pltpu.VMEM((3, T, HEAD_DIM), F32)]),
        out_shape=jax.ShapeDtypeStruct((S, W_A), F32),
        compiler_params=_cparams(("parallel", "parallel")),
        name="attn",
    )(slopes, proj, proj, proj, proj, proj)


def _softplus(x):
    return jnp.maximum(x, 0.0) + jnp.log1p(jnp.exp(-jnp.abs(x)))


def _delta_kernel(hist_ref, qkv_ref, z_ref, ba_ref, cw_ref, alog_ref, dtb_ref, gout_ref, o_ref,
                  ext, act, gct, state):
    i = pl.program_id(0)
    CB = DELTA_BLOCK
    C = DELTA_CHUNK
    H = N_HEADS_B
    D = HEAD_DIM

    @pl.when(i == 0)
    def _():
        state[...] = jnp.zeros_like(state)
        ext[0:SUBLANES, :] = jnp.zeros((SUBLANES, 3 * W_B), F32)

    @pl.when(i > 0)
    def _():
        ext[0:SUBLANES, :] = hist_ref[...]

    ext[SUBLANES:SUBLANES + CB, :] = qkv_ref[...]

    for c in range(3 * H):
        sl = slice(c * D, (c + 1) * D)
        acc = cw_ref[CONV_WIDTH - 1:CONV_WIDTH, sl] * ext[SUBLANES:SUBLANES + CB, sl]
        for t in range(1, CONV_WIDTH):
            acc = acc + cw_ref[CONV_WIDTH - 1 - t:CONV_WIDTH - t, sl] * ext[SUBLANES - t:SUBLANES - t + CB, sl]
        a = acc * jax.nn.sigmoid(acc)
        if c < 2 * H:
            a = a * lax.rsqrt(jnp.sum(a * a, axis=-1, keepdims=True) + EPS)
        if c < H:
            a = a * (D ** -0.5)
        act[:, sl] = a

    ba = ba_ref[...]
    beta = jax.nn.sigmoid(ba)
    g = -jnp.exp(alog_ref[...]) * _softplus(ba + dtb_ref[...])
    row = lax.broadcasted_iota(I32, (CB, CB), 0)
    col = lax.broadcasted_iota(I32, (CB, CB), 1)
    chunk_tri = jnp.where((row // C == col // C) & (row >= col), 1.0, 0.0).astype(F32)
    gc = jnp.dot(chunk_tri, g, precision=HIGHEST, preferred_element_type=F32)
    gct[...] = gc.T
    eg = jnp.exp(gc)

    ri = lax.broadcasted_iota(I32, (C, C), 0)
    ci = lax.broadcasted_iota(I32, (C, C), 1)
    causal = ri >= ci
    strict = ri > ci
    gout = gout_ref[...]

    heads = range(H)
    for n in range(CB // C):
        rows = slice(n * C, (n + 1) * C)
        q = [act[rows, h * D:(h + 1) * D] for h in heads]
        k = [act[rows, (H + h) * D:(H + h + 1) * D] for h in heads]
        v = [act[rows, (2 * H + h) * D:(2 * H + h + 1) * D] for h in heads]
        beta_c = [beta[rows, h:h + 1] for h in heads]
        gc_c = [gc[rows, H + h:H + h + 1] for h in heads]
        eg_c = [eg[rows, H + h:H + h + 1] for h in heads]
        gc_r = [gct[H + h:H + h + 1, rows] for h in heads]
        g_last = [gc_r[h][:, C - 1:C] for h in heads]
        decay = [jnp.exp(jnp.where(causal, gc_c[h] - gc_r[h], NEG)) for h in heads]
        kb = [k[h] * beta_c[h] for h in heads]
        kq = [_dot_nt(jnp.concatenate([kb[h], q[h]], axis=0), k[h]) for h in heads]
        qk = [jnp.where(causal, kq[h][C:] * decay[h], 0.0) for h in heads]
        pw = [jnp.where(strict, -kq[h][:C] * decay[h], 0.0) for h in heads]
        r = pw
        pw = [_dot(pw[h], pw[h]) for h in heads]
        for _ in range(int(np.log2(C)) - 2):
            pr = [_dot(jnp.concatenate([pw[h], r[h]], axis=0), pw[h]) for h in heads]
            r = [r[h] + pw[h] + pr[h][C:] for h in heads]
            pw = [pr[h][:C] for h in heads]
        r = [r[h] + pw[h] + _dot(r[h], pw[h]) for h in heads]
        bmat = [jnp.concatenate([v[h] * beta_c[h], kb[h] * eg_c[h]], axis=-1) for h in heads]
        uw = [bmat[h] + _dot(r[h], bmat[h]) for h in heads]
        wq = [jnp.concatenate([uw[h][:, D:], q[h] * eg_c[h]], axis=0) for h in heads]
        kd = [k[h] * jnp.exp(g_last[h] - gc_c[h]) for h in heads]
        s_old = [state[h] for h in heads]
        ws = [_dot(wq[h], s_old[h]) for h in heads]
        v_new = [uw[h][:, :D] - ws[h][:C] for h in heads]
        for h in heads:
            state[h] = s_old[h] * jnp.exp(g_last[h]) + _dot_tn(kd[h], v_new[h])
        o = [ws[h][C:] + _dot(qk[h], v_new[h]) for h in heads]
        for h in heads:
            z = z_ref[rows, h * D:(h + 1) * D]
            o_ref[rows, h * D:(h + 1) * D] = _rms(o[h], gout) * (z * jax.nn.sigmoid(z))


def _gated_deltanet(proj, ba, conv_w, alog_pad, dtb_pad, g_out):
    S = proj.shape[0]
    CB = DELTA_BLOCK
    qkv_col = W_A * 3 // (3 * W_B)
    z_col = (3 * W_A + 3 * W_B) // W_B
    return pl.pallas_call(
        _delta_kernel,
        grid=(S // CB,),
        in_specs=[pl.BlockSpec((SUBLANES, 3 * W_B), lambda i: (jnp.maximum(i * (CB // SUBLANES) - 1, 0), qkv_col)),
                  pl.BlockSpec((CB, 3 * W_B), lambda i: (i, qkv_col)),
                  pl.BlockSpec((CB, W_B), lambda i: (i, z_col)),
                  pl.BlockSpec((CB, LANES), lambda i: (i, 0)),
                  pl.BlockSpec((CONV_WIDTH, 3 * W_B), lambda i: (0, 0)),
                  pl.BlockSpec((1, LANES), lambda i: (0, 0)),
                  pl.BlockSpec((1, LANES), lambda i: (0, 0)),
                  pl.BlockSpec((1, HEAD_DIM), lambda i: (0, 0))],
        out_specs=pl.BlockSpec((CB, W_B), lambda i: (i, 0)),
        out_shape=jax.ShapeDtypeStruct((S, W_B), F32),
        scratch_shapes=[pltpu.VMEM((CB + SUBLANES, 3 * W_B), F32),
                        pltpu.VMEM((CB, 3 * W_B), F32),
                        pltpu.VMEM((LANES, CB), F32),
                        pltpu.VMEM((N_HEADS_B, HEAD_DIM, HEAD_DIM), F32)],
        compiler_params=_cparams(("arbitrary",)),
        name="delta",
    )(proj, proj, proj, ba, conv_w, alog_pad, dtb_pad, g_out)


def _memkv_kernel(mem_ref, g_ref, w_ref, k_ref, v_ref):
    kv = _dot(_rms(mem_ref[...], g_ref[...]), w_ref[...])
    k_ref[...] = kv[:, :W_MEM].astype(BF16)
    v_ref[...] = kv[:, W_MEM:].astype(BF16)


def _memkv(mem, g, w_kv):
    M = mem.shape[0]
    return pl.pallas_call(
        _memkv_kernel,
        out_shape=[jax.ShapeDtypeStruct((M, W_MEM), BF16), jax.ShapeDtypeStruct((M, W_MEM), BF16)],
        compiler_params=pltpu.CompilerParams(vmem_limit_bytes=VMEM_LIMIT),
        name="memkv",
    )(mem, g, w_kv)


def _mix_kernel(attn_ref, delta_ref, x_ref, gattn_ref, wout_ref, gcross_ref, wq_ref, km_ref, vm_ref, wo_ref,
                gmoe_ref, wr_ref, br_ref, h2_ref, hn_ref, eid_ref, wts_ref):
    an = _rms(attn_ref[...], gattn_ref[...])
    mix = jnp.concatenate([an.astype(BF16), delta_ref[...].astype(BF16)], axis=-1)
    h1 = x_ref[...] + jnp.dot(mix, wout_ref[...], preferred_element_type=F32)

    q = _dot(_rms(h1, gcross_ref[...]), wq_ref[...]) * (HEAD_DIM ** -0.5)
    outs = []
    for hh in range(N_MEM_HEADS):
        sl = slice(hh * HEAD_DIM, (hh + 1) * HEAD_DIM)
        s = _dot_nt(q[:, sl], km_ref[:, sl])
        e = jnp.exp(s - jnp.max(s, axis=-1, keepdims=True))
        outs.append(_dot(e, vm_ref[:, sl]) / jnp.sum(e, axis=-1, keepdims=True))
    h2 = h1 + _dot(jnp.concatenate(outs, axis=-1), wo_ref[...])
    h2_ref[...] = h2

    hn = _rms(h2, gmoe_ref[...])
    hn_ref[...] = hn
    logits = _dot_x3(hn, wr_ref) + br_ref[...]
    lane = lax.broadcasted_iota(I32, logits.shape, 1)
    gl = jnp.where(lane < N_GROUPS, logits, NEG)
    gmax = jnp.max(gl, axis=-1, keepdims=True)
    g_sel = jnp.min(jnp.where(gl == gmax, lane, LANES), axis=-1, keepdims=True)
    g_gate = 1.0 / jnp.sum(jnp.exp(gl - gmax), axis=-1, keepdims=True)
    in_group = (lane >= N_GROUPS) & ((lane - N_GROUPS) // EXPERTS_PER_GROUP == g_sel)
    el = jnp.where(in_group, logits, NEG)
    v1 = jnp.max(el, axis=-1, keepdims=True)
    i1 = jnp.min(jnp.where(in_group & (el == v1), lane, LANES), axis=-1, keepdims=True)
    in_rest = in_group & (lane != i1)
    el2 = jnp.where(in_rest, logits, NEG)
    v2 = jnp.max(el2, axis=-1, keepdims=True)
    i2 = jnp.min(jnp.where(in_rest & (el2 == v2), lane, LANES), axis=-1, keepdims=True)
    e2 = jnp.exp(v2 - v1)
    w1 = g_gate / (1.0 + e2)
    w2 = g_gate * e2 / (1.0 + e2)
    eid_ref[...] = jnp.where(lane == 0, i1 - N_GROUPS, jnp.where(lane == 1, i2 - N_GROUPS, 0))
    wts_ref[...] = jnp.where(lane == 0, w1, jnp.where(lane == 1, w2, 0.0))


def _mix(attn, delta, x, g_attn, w_out, g_cross, w_q, k_mem, v_mem, w_o, g_moe, w_router, b_router, tm=512):
    S, D = x.shape
    row = lambda w: pl.BlockSpec((tm, w), lambda i: (i, 0))
    full = lambda a: pl.BlockSpec(a.shape, lambda i: (0, 0), pipeline_mode=pl.Buffered(1))
    return pl.pallas_call(
        _mix_kernel,
        grid=(S // tm,),
        in_specs=[row(W_A), row(W_B), row(D), full(g_attn), full(w_out), full(g_cross), full(w_q),
                  full(k_mem), full(v_mem), full(w_o), full(g_moe), full(w_router), full(b_router)],
        out_specs=[row(D), row(D), row(LANES), row(LANES)],
        out_shape=[jax.ShapeDtypeStruct((S, D), F32), jax.ShapeDtypeStruct((S, D), F32),
                   jax.ShapeDtypeStruct((S, LANES), I32), jax.ShapeDtypeStruct((S, LANES), F32)],
        compiler_params=_cparams(("parallel",)),
        name="mix",
    )(attn, delta, x, g_attn, w_out, g_cross, w_q, k_mem, v_mem, w_o, g_moe, w_router, b_router)


def _rank_kernel(eid_ref, dest_ref, cnt_ref, carry, pstart):
    ph = pl.program_id(0)
    i = pl.program_id(1)
    tm = eid_ref.shape[0]
    lane = lax.broadcasted_iota(I32, (tm, LANES), 1)
    e = eid_ref[...]
    e1 = e[:, 0:1]
    e2 = e[:, 1:2]
    onehot = jnp.where((lane == e1) | (lane == e2), 1.0, 0.0).astype(F32)
    colsum = jnp.sum(onehot, axis=0, keepdims=True)

    @pl.when(jnp.logical_and(ph == 0, i == 0))
    def _():
        carry[...] = jnp.zeros_like(carry)

    @pl.when(jnp.logical_and(ph == 1, i == 0))
    def _():
        cnt = carry[...]
        cnt_ref[...] = cnt
        padded = jnp.floor((cnt + (MOE_BLOCK - 1)) * (1.0 / MOE_BLOCK)) * MOE_BLOCK
        lane8 = lax.broadcasted_iota(I32, cnt.shape, 1)
        incl = padded
        shift = 1
        while shift < LANES:
            incl = incl + jnp.where(lane8 >= shift, pltpu.roll(incl, shift, 1), 0.0)
            shift *= 2
        pstart[...] = incl - padded
        carry[...] = jnp.zeros_like(carry)

    @pl.when(ph == 1)
    def _():
        r = lax.broadcasted_iota(I32, (tm, tm), 0)
        c = lax.broadcasted_iota(I32, (tm, tm), 1)
        before = jnp.where(r > c, 1.0, 0.0).astype(BF16)
        pos = (jnp.dot(before, onehot.astype(BF16), preferred_element_type=F32)
               + carry[0:1, :] + pstart[0:1, :])
        d1 = jnp.sum(jnp.where(lane == e1, pos, 0.0), axis=-1, keepdims=True)
        d2 = jnp.sum(jnp.where(lane == e2, pos, 0.0), axis=-1, keepdims=True)
        dest = jnp.where(lane == 0, d1, jnp.where(lane == 1, d2, 0.0))
        dest_ref[...] = dest.T[:SUBLANES].astype(I32)

    carry[...] = carry[...] + colsum


def _rank(eid, tm=256):
    S = eid.shape[0]
    return pl.pallas_call(
        _rank_kernel,
        grid=(2, S // tm),
        in_specs=[pl.BlockSpec((tm, LANES), lambda p, i: (i, 0))],
        out_specs=[pl.BlockSpec((SUBLANES, tm), lambda p, i: (0, i * p)),
                   pl.BlockSpec((SUBLANES, LANES), lambda p, i: (0, 0))],
        out_shape=[jax.ShapeDtypeStruct((SUBLANES, S), I32), jax.ShapeDtypeStruct((SUBLANES, LANES), F32)],
        scratch_shapes=[pltpu.VMEM((SUBLANES, LANES), F32), pltpu.VMEM((SUBLANES, LANES), F32)],
        compiler_params=_cparams(("arbitrary", "arbitrary")),
        name="rank",
    )(eid)


def _scatter_kernel(dest_ref, nval_ref, hn_ref, xb_ref, zeros, sem):
    i = pl.program_id(0)
    tm = hn_ref.shape[0]

    @pl.when(i == 0)
    def _():
        zeros[...] = jnp.zeros_like(zeros)

        def zero_block(b):
            return pltpu.make_async_copy(zeros, xb_ref.at[pl.ds(b * MOE_BLOCK, MOE_BLOCK), :], sem)

        def zissue(b, carry):
            @pl.when(nval_ref[b] < MOE_BLOCK)
            def _():
                zero_block(b).start()
            return carry

        def zdrain(b, carry):
            @pl.when(nval_ref[b] < MOE_BLOCK)
            def _():
                zero_block(b).wait()
            return carry

        lax.fori_loop(0, nval_ref.shape[0], zissue, 0)
        lax.fori_loop(0, nval_ref.shape[0], zdrain, 0)

    n_tok = pl.num_programs(0) * tm

    def issue(r, carry):
        for k in range(TOP_K):
            d = dest_ref[k * n_tok + i * tm + r]
            pltpu.make_async_copy(hn_ref.at[pl.ds(r, 1), :], xb_ref.at[pl.ds(d, 1), :], sem).start()
        return carry

    lax.fori_loop(0, tm, issue, 0, unroll=ROW_DMA_UNROLL)
    for k in range(TOP_K):
        pltpu.make_async_copy(hn_ref, xb_ref.at[pl.ds(0, tm), :], sem).wait()


def _scatter_rows(dest_flat, block_valid, hn, n_rows, tm=512):
    S, D = hn.shape
    return pl.pallas_call(
        _scatter_kernel,
        grid_spec=pltpu.PrefetchScalarGridSpec(
            num_scalar_prefetch=2,
            grid=(S // tm,),
            in_specs=[pl.BlockSpec((tm, D), lambda i, d, nv: (i, 0))],
            out_specs=pl.BlockSpec(memory_space=pl.ANY),
            scratch_shapes=[pltpu.VMEM((MOE_BLOCK, D), F32), pltpu.SemaphoreType.DMA(())]),
        out_shape=jax.ShapeDtypeStruct((n_rows, D), F32),
        compiler_params=_cparams(("arbitrary",)),
        name="scatter",
    )(dest_flat, block_valid, hn)


def _expert_kernel(bexp_ref, nval_ref, slot_ref, next_ref, x_ref, wg_ref, wu_ref, wd_ref, y_ref,
                   wg_f, wu_f, wd_f, wg_s, wu_s, wd_s, sem):
    b = pl.program_id(0)
    e = bexp_ref[b]
    e_prev = bexp_ref[jnp.maximum(b - 1, 0)]
    nval = nval_ref[b]

    def weight_copies(ex, slot):
        copies = []
        for w, (src, dst) in enumerate(((wg_ref, wg_f), (wu_ref, wu_f), (wd_ref, wd_f))):
            rows = src.shape[1] // EXPERT_DMA_CHUNKS
            for c in range(EXPERT_DMA_CHUNKS):
                sl = pl.ds(c * rows, rows)
                copies.append(pltpu.make_async_copy(src.at[ex, sl, :], dst.at[slot, sl, :], sem.at[slot, w]))
        return copies

    @pl.when(jnp.logical_and(b == 0, nval > 0))
    def _():
        for c in weight_copies(e, slot_ref[e]):
            c.start()

    @pl.when(jnp.logical_and(nval > 0, jnp.logical_or(b == 0, e != e_prev)))
    def _():
        slot = slot_ref[e]
        for c in weight_copies(e, slot):
            c.wait()
        nxt = next_ref[e]

        @pl.when(nxt >= 0)
        def _():
            for c in weight_copies(nxt, 1 - slot):
                c.start()

        wg_s[...] = wg_f[slot].astype(BF16)
        wu_s[...] = wu_f[slot].astype(BF16)
        wd_s[...] = wd_f[slot].astype(BF16)

    @pl.when(nval == 0)
    def _():
        y_ref[...] = jnp.zeros_like(y_ref)

    @pl.when(nval > 0)
    def _():
        x = x_ref[...].astype(BF16)
        gate = jnp.dot(x, wg_s[...], preferred_element_type=F32)
        up = jnp.dot(x, wu_s[...], preferred_element_type=F32)
        hid = gate * jax.nn.sigmoid(gate) * up
        y_ref[...] = jnp.dot(hid.astype(BF16), wd_s[...], preferred_element_type=F32)


def _experts(block_expert, block_valid, expert_slot, expert_next, xb, w_gate, w_up, w_down):
    P, D = xb.shape
    DE = w_gate.shape[-1]
    nb = P // MOE_BLOCK
    row_blk = pl.BlockSpec((MOE_BLOCK, D), lambda b, *_: (b, 0))
    hbm = pl.BlockSpec(memory_space=pl.ANY)
    return pl.pallas_call(
        _expert_kernel,
        grid_spec=pltpu.PrefetchScalarGridSpec(
            num_scalar_prefetch=4,
            grid=(nb,),
            in_specs=[row_blk, hbm, hbm, hbm],
            out_specs=row_blk,
            scratch_shapes=[pltpu.VMEM((2, D, DE), F32), pltpu.VMEM((2, D, DE), F32), pltpu.VMEM((2, DE, D), F32),
                            pltpu.VMEM((D, DE), BF16), pltpu.VMEM((D, DE), BF16), pltpu.VMEM((DE, D), BF16),
                            pltpu.SemaphoreType.DMA((2, 3))]),
        out_shape=jax.ShapeDtypeStruct((P, D), F32),
        compiler_params=_cparams(("arbitrary",)),
        name="experts",
    )(block_expert, block_valid, expert_slot, expert_next, xb, w_gate, w_up, w_down)


def _combine_kernel(dest_ref, h2_ref, wts_ref, g_ref, yb_ref, o_ref, buf, sem):
    i = pl.program_id(0)
    n = pl.num_programs(0)
    tm = h2_ref.shape[0]
    n_tok = n * tm

    def gather_block(blk, slot):
        def issue(r, carry):
            for k in range(TOP_K):
                d = dest_ref[k * n_tok + blk * tm + r]
                pltpu.make_async_copy(yb_ref.at[pl.ds(d, 1), :], buf.at[slot, k, pl.ds(r, 1), :], sem.at[slot]).start()
            return carry

        lax.fori_loop(0, tm, issue, 0, unroll=ROW_DMA_UNROLL)

    @pl.when(i == 0)
    def _():
        gather_block(0, 0)

    @pl.when(i + 1 < n)
    def _():
        gather_block(i + 1, (i + 1) % 2)

    slot = i % 2
    for k in range(TOP_K):
        pltpu.make_async_copy(yb_ref.at[pl.ds(0, tm), :], buf.at[slot, k], sem.at[slot]).wait()
    wts = wts_ref[...]
    h = h2_ref[...] + wts[:, 0:1] * buf[slot, 0] + wts[:, 1:2] * buf[slot, 1]
    o_ref[...] = _rms(h, g_ref[...])


def _combine(dest_flat, h2, wts, g_final, yb, tm=256):
    S, D = h2.shape
    return pl.pallas_call(
        _combine_kernel,
        grid_spec=pltpu.PrefetchScalarGridSpec(
            num_scalar_prefetch=1,
            grid=(S // tm,),
            in_specs=[pl.BlockSpec((tm, D), lambda i, d: (i, 0)),
                      pl.BlockSpec((tm, LANES), lambda i, d: (i, 0)),
                      pl.BlockSpec((1, D), lambda i, d: (0, 0)),
                      pl.BlockSpec(memory_space=pl.ANY)],
            out_specs=pl.BlockSpec((tm, D), lambda i, d: (i, 0)),
            scratch_shapes=[pltpu.VMEM((2, TOP_K, tm, D), F32), pltpu.SemaphoreType.DMA((2,))]),
        out_shape=jax.ShapeDtypeStruct((S, D), F32),
        compiler_params=_cparams(("arbitrary",)),
        name="combine",
    )(dest_flat, h2, wts, g_final, yb)


def _lane_pad(v, offset):
    return jnp.zeros((1, LANES), F32).at[0, offset:offset + v.shape[0]].set(v.astype(F32))


def _layer(h, mem, g_mix, w_in, conv_w, a_log, dt_bias, g_delta_out, g_attn_out, w_out, g_cross, g_mem,
           w_q_mem, w_kv_mem, w_o_mem, g_moe, w_group, b_group, w_expert, b_expert, w_gate, w_up, w_down,
           g_final):
    S, D = h.shape
    H = N_HEADS_B
    w_main = w_in[:, :W_MAIN].astype(BF16)
    w_small = _split_hi_lo(jnp.pad(w_in[:, W_MAIN:], ((0, 0), (0, LANES - 2 * H))))
    proj, ba = _inproj(h, g_mix[None], w_main, w_small)

    slopes = jnp.asarray(2.0 ** (-8.0 * np.arange(1, N_HEADS_A + 1) / N_HEADS_A), dtype=F32)
    attn = _dilated_attention(proj, slopes)
    delta = _gated_deltanet(proj, ba, conv_w, _lane_pad(a_log, H), _lane_pad(dt_bias, H), g_delta_out[None])

    k_mem, v_mem = _memkv(mem, g_mem[None], w_kv_mem.astype(BF16))
    w_router = _split_hi_lo(jnp.pad(jnp.concatenate([w_group, w_expert], axis=1),
                                    ((0, 0), (0, LANES - N_GROUPS - N_EXPERTS))))
    b_router = _lane_pad(jnp.concatenate([b_group, b_expert]), 0)
    h2, hn, eid, wts = _mix(attn, delta, h, g_attn_out[None], w_out.astype(BF16), g_cross[None],
                            w_q_mem.astype(BF16), k_mem, v_mem, w_o_mem.astype(BF16), g_moe[None],
                            w_router, b_router)

    dest, cnt = _rank(eid)
    dest_flat = dest[:TOP_K].reshape(TOP_K * S)
    counts = cnt[0, :N_EXPERTS].astype(I32)
    padded = ((counts + MOE_BLOCK - 1) // MOE_BLOCK) * MOE_BLOCK
    pend = jnp.cumsum(padded)
    n_rows = S * TOP_K + N_EXPERTS * MOE_BLOCK
    block_row = jnp.arange(n_rows // MOE_BLOCK, dtype=I32) * MOE_BLOCK
    block_expert = jnp.minimum(jnp.sum((block_row[:, None] >= pend[None, :]).astype(I32), axis=1), N_EXPERTS - 1)
    seg_end = (pend - padded + counts)[block_expert]
    block_valid = jnp.clip(seg_end - block_row, 0, MOE_BLOCK).astype(I32)

    present = counts > 0
    expert_slot = ((jnp.cumsum(present.astype(I32)) - 1) % 2).astype(I32)
    eidx = jnp.where(present, jnp.arange(N_EXPERTS, dtype=I32), N_EXPERTS)
    at_or_after = lax.cummin(eidx, reverse=True)
    expert_next = jnp.concatenate([at_or_after[1:], jnp.full((1,), N_EXPERTS, I32)])
    expert_next = jnp.where(expert_next < N_EXPERTS, expert_next, -1).astype(I32)

    xb = _scatter_rows(dest_flat, block_valid, hn, n_rows)
    yb = _experts(block_expert, block_valid, expert_slot, expert_next, xb, w_gate, w_up, w_down)
    return _combine(dest_flat, h2, wts, g_final[None], yb)


def kernel(x, mem, g_mix, w_in, conv_w, a_log, dt_bias, g_delta_out, g_attn_out, w_out, g_cross, g_mem, w_q_mem,
           w_kv_mem, w_o_mem, g_moe, w_group, b_group, w_expert, b_expert, w_gate, w_up, w_down, g_final):
    assert x.shape[0] == 1 and mem.shape[0] == 1 and g_mix.shape[0] == 1
    out = _layer(x[0].astype(F32), mem[0].astype(F32), g_mix[0], w_in[0], conv_w[0], a_log[0], dt_bias[0],
                 g_delta_out[0], g_attn_out[0], w_out[0], g_cross[0], g_mem[0], w_q_mem[0], w_kv_mem[0],
                 w_o_mem[0], g_moe[0], w_group[0], b_group[0], w_expert[0], b_expert[0], w_gate[0], w_up[0],
                 w_down[0], g_final)
    return out[None].astype(x.dtype)
```

```python
import functools

import numpy as np
import jax
import jax.numpy as jnp
from jax import lax
from jax.experimental import pallas as pl
from jax.experimental.pallas import tpu as pltpu

F32 = jnp.float32
BF16 = jnp.bfloat16
I32 = jnp.int32
HIGHEST = lax.Precision.HIGHEST

EPS = 1e-6
HEAD_DIM = 128
LANES = 128
SUBLANES = 8
NEG = -1e30

N_HEADS_A = 8
N_HEADS_B = 8
W_A = N_HEADS_A * HEAD_DIM
W_B = N_HEADS_B * HEAD_DIM
W_MAIN = 3 * W_A + 4 * W_B
DILATED_PATTERNS = ((128, 1), (512, 4), (2048, 16))
ATTN_BLOCK = 128
ATTN_TILE = 2048
CONV_WIDTH = 4
DELTA_CHUNK = 64
DELTA_BLOCK = 256
N_MEM_HEADS = 4
W_MEM = N_MEM_HEADS * HEAD_DIM
N_GROUPS = 8
EXPERTS_PER_GROUP = 8
N_EXPERTS = 64
TOP_K = 2
MOE_BLOCK = 128
EXPERT_DMA_CHUNKS = 8
ROW_DMA_UNROLL = 8

VMEM_LIMIT = 56 * 1024 * 1024


def _cparams(sem):
    return pltpu.CompilerParams(dimension_semantics=sem, vmem_limit_bytes=VMEM_LIMIT)


def _rms(x, g):
    return x * lax.rsqrt(jnp.mean(x * x, axis=-1, keepdims=True) + EPS) * g


def _dot(a, b):
    return jnp.dot(a.astype(BF16), b.astype(BF16), preferred_element_type=F32)


def _dot_nt(a, b):
    return lax.dot_general(a.astype(BF16), b.astype(BF16), (((1,), (1,)), ((), ())), preferred_element_type=F32)


def _dot_tn(a, b):
    return lax.dot_general(a.astype(BF16), b.astype(BF16), (((0,), (0,)), ((), ())), preferred_element_type=F32)


def _split_hi_lo(w):
    hi = w.astype(BF16)
    lo = (w - hi.astype(F32)).astype(BF16)
    return jnp.concatenate([hi, lo], axis=1)


def _dot_x3(x, w_hl_ref):
    x_hi = x.astype(BF16)
    x_lo = (x - x_hi.astype(F32)).astype(BF16)
    r = jnp.dot(x_hi, w_hl_ref[...], preferred_element_type=F32)
    return r[:, :LANES] + r[:, LANES:] + jnp.dot(x_lo, w_hl_ref[:, :LANES], preferred_element_type=F32)


def _castw_kernel(w_ref, ws_ref, o_ref, os_ref):
    o_ref[...] = w_ref[...].astype(BF16)

    @pl.when(pl.program_id(0) == 0)
    def _():
        lane = lax.broadcasted_iota(I32, ws_ref.shape, 1)
        w = jnp.where(lane < 2 * N_HEADS_B, ws_ref[...], 0.0)
        hi = w.astype(BF16)
        os_ref[:, :LANES] = hi
        os_ref[:, LANES:] = (w - hi.astype(F32)).astype(BF16)


def _cast_w_in(w_in, tn=1024):
    D = w_in.shape[0]
    assert w_in.shape[1] == W_MAIN + 2 * N_HEADS_B and W_MAIN % tn == 0
    return pl.pallas_call(
        _castw_kernel,
        grid=(W_MAIN // tn,),
        in_specs=[pl.BlockSpec((D, tn), lambda j: (0, j)),
                  pl.BlockSpec((D, LANES), lambda j: (0, W_MAIN // LANES))],
        out_specs=[pl.BlockSpec((D, tn), lambda j: (0, j)),
                   pl.BlockSpec((D, 2 * LANES), lambda j: (0, 0))],
        out_shape=[jax.ShapeDtypeStruct((D, W_MAIN), BF16), jax.ShapeDtypeStruct((D, 2 * LANES), BF16)],
        compiler_params=_cparams(("arbitrary",)),
        name="castw",
    )(w_in, w_in)


def _inproj_kernel(x_ref, g_ref, w_ref, ws_ref, o_ref, os_ref, u_scr):
    @pl.when(pl.program_id(1) == 0)
    def _():
        u = _rms(x_ref[...], g_ref[...])
        u_scr[...] = u.astype(BF16)
        os_ref[...] = _dot_x3(u, ws_ref)

    o_ref[...] = jnp.dot(u_scr[...], w_ref[...], preferred_element_type=F32)


def _inproj(x, g, w_main, w_small, tm=1024, tn=1024):
    S, D = x.shape
    N = w_main.shape[1]
    return pl.pallas_call(
        _inproj_kernel,
        grid=(S // tm, N // tn),
        in_specs=[pl.BlockSpec((tm, D), lambda i, j: (i, 0)),
                  pl.BlockSpec((1, D), lambda i, j: (0, 0)),
                  pl.BlockSpec((D, tn), lambda i, j: (0, j)),
                  pl.BlockSpec((D, 2 * LANES), lambda i, j: (0, 0))],
        out_specs=[pl.BlockSpec((tm, tn), lambda i, j: (i, j)),
                   pl.BlockSpec((tm, LANES), lambda i, j: (i, 0))],
        out_shape=[jax.ShapeDtypeStruct((S, N), F32), jax.ShapeDtypeStruct((S, LANES), F32)],
        scratch_shapes=[pltpu.VMEM((tm, D), BF16)],
        compiler_params=_cparams(("parallel", "arbitrary")),
        name="inproj",
    )(x, g, w_main, w_small)


def _attn_kernel(slope_ref, q_ref, kp_ref, kc_ref, vp_ref, vc_ref, o_ref, kk, vv, o_scr, l_scr):
    i = pl.program_id(0)
    h = pl.program_id(1)
    T = ATTN_TILE
    B = ATTN_BLOCK
    slope = slope_ref[h]
    kk[0:T, :] = kp_ref[...]
    kk[T:2 * T, :] = kc_ref[...]
    vv[0:T, :] = vp_ref[...]
    vv[T:2 * T, :] = vc_ref[...]
    qi = lax.broadcasted_iota(I32, (B, 2 * B), 0)
    ki = lax.broadcasted_iota(I32, (B, 2 * B), 1)
    dist = qi + B - ki
    band = (dist >= 0) & (dist <= B)
    band_first = band & (ki >= B)
    distf = dist.astype(F32)
    scale = HEAD_DIM ** -0.5

    for p, (window, d) in enumerate(DILATED_PATTERNS):
        assert window // d == B and T % (B * d) == 0
        bias = jnp.where(band, distf * (-slope * d), NEG)
        bias_first = jnp.where(band_first, distf * (-slope * d), NEG)

        def body(b, carry, p=p, d=d, bias=bias, bias_first=bias_first):
            r = b % d
            j = b // d
            qs = r + B * d * j
            ks = T - B * d + qs
            q = q_ref[pl.ds(qs, B, stride=d), :] * scale
            k = kk[pl.ds(ks, 2 * B, stride=d), :]
            v = vv[pl.ds(ks, 2 * B, stride=d), :]
            s = _dot_nt(q, k)
            first = jnp.logical_and(i == 0, j == 0)
            s = s + jnp.where(first, bias_first, bias)
            m = jnp.max(s, axis=-1, keepdims=True)
            e = jnp.exp(s - m)
            l = jnp.sum(e, axis=-1, keepdims=True)
            o = _dot(e, v) / l
            lse = m + jnp.log(l)
            o_scr[p, pl.ds(qs, B, stride=d), :] = o
            l_scr[p, pl.ds(qs, B, stride=d), :] = jnp.broadcast_to(lse, (B, HEAD_DIM))
            return carry

        lax.fori_loop(0, T // B, body, 0, unroll=8)

    l0, l1, l2 = l_scr[0], l_scr[1], l_scr[2]
    m = jnp.maximum(jnp.maximum(l0, l1), l2)
    w0, w1, w2 = jnp.exp(l0 - m), jnp.exp(l1 - m), jnp.exp(l2 - m)
    o_ref[...] = (w0 * o_scr[0] + w1 * o_scr[1] + w2 * o_scr[2]) / (w0 + w1 + w2)


def _dilated_attention(proj, slopes):
    S = proj.shape[0]
    T = ATTN_TILE
    H = N_HEADS_A
    blk = (T, HEAD_DIM)
    return pl.pallas_call(
        _attn_kernel,
        grid_spec=pltpu.PrefetchScalarGridSpec(
            num_scalar_prefetch=1,
            grid=(S // T, H),
            in_specs=[pl.BlockSpec(blk, lambda i, h, s: (i, h)),
                      pl.BlockSpec(blk, lambda i, h, s: (jnp.maximum(i - 1, 0), H + h)),
                      pl.BlockSpec(blk, lambda i, h, s: (i, H + h)),
                      pl.BlockSpec(blk, lambda i, h, s: (jnp.maximum(i - 1, 0), 2 * H + h)),
                      pl.BlockSpec(blk, lambda i, h, s: (i, 2 * H + h))],
            out_specs=pl.BlockSpec(blk, lambda i, h, s: (i, h)),
            scratch_shapes=[pltpu.VMEM((2 * T, HEAD_DIM), F32), pltpu.VMEM((2 * T, HEAD_DIM), F32),
                            pltpu.VMEM((3, T, HEAD_DIM), F32), pltpu.VMEM((3, T, HEAD_DIM), F32)]),
        out_shape=jax.ShapeDtypeStruct((S, W_A), F32),
        compiler_params=_cparams(("parallel", "parallel")),
        name="attn",
    )(slopes, proj, proj, proj, proj, proj)


def _softplus(x):
    return jnp.maximum(x, 0.0) + jnp.log1p(jnp.exp(-jnp.abs(x)))


def _delta_kernel(hist_ref, qkv_ref, z_ref, ba_ref, cw_ref, alog_ref, dtb_ref, gout_ref, o_ref,
                  ext, act, gct, state):
    i = pl.program_id(0)
    CB = DELTA_BLOCK
    C = DELTA_CHUNK
    H = N_HEADS_B
    D = HEAD_DIM

    @pl.when(i == 0)
    def _():
        state[...] = jnp.zeros_like(state)
        ext[0:SUBLANES, :] = jnp.zeros((SUBLANES, 3 * W_B), F32)

    @pl.when(i > 0)
    def _():
        ext[0:SUBLANES, :] = hist_ref[...]

    ext[SUBLANES:SUBLANES + CB, :] = qkv_ref[...]

    for c in range(3 * H):
        sl = slice(c * D, (c + 1) * D)
        acc = cw_ref[CONV_WIDTH - 1:CONV_WIDTH, sl] * ext[SUBLANES:SUBLANES + CB, sl]
        for t in range(1, CONV_WIDTH):
            acc = acc + cw_ref[CONV_WIDTH - 1 - t:CONV_WIDTH - t, sl] * ext[SUBLANES - t:SUBLANES - t + CB, sl]
        a = acc * jax.nn.sigmoid(acc)
        if c < 2 * H:
            a = a * lax.rsqrt(jnp.sum(a * a, axis=-1, keepdims=True) + EPS)
        if c < H:
            a = a * (D ** -0.5)
        act[:, sl] = a

    ba = ba_ref[...]
    beta = jax.nn.sigmoid(ba)
    g = -jnp.exp(alog_ref[...]) * _softplus(ba + dtb_ref[...])
    row = lax.broadcasted_iota(I32, (CB, CB), 0)
    col = lax.broadcasted_iota(I32, (CB, CB), 1)
    chunk_tri = jnp.where((row // C == col // C) & (row >= col), 1.0, 0.0).astype(F32)
    gc = jnp.dot(chunk_tri, g, precision=HIGHEST, preferred_element_type=F32)
    gct[...] = gc.T
    eg = jnp.exp(gc)

    ri = lax.broadcasted_iota(I32, (C, C), 0)
    ci = lax.broadcasted_iota(I32, (C, C), 1)
    causal = ri >= ci
    strict = ri > ci
    gout = gout_ref[...]

    heads = range(H)
    for n in range(CB // C):
        rows = slice(n * C, (n + 1) * C)
        q = [act[rows, h * D:(h + 1) * D] for h in heads]
        k = [act[rows, (H + h) * D:(H + h + 1) * D] for h in heads]
        v = [act[rows, (2 * H + h) * D:(2 * H + h + 1) * D] for h in heads]
        beta_c = [beta[rows, h:h + 1] for h in heads]
        gc_c = [gc[rows, H + h:H + h + 1] for h in heads]
        eg_c = [eg[rows, H + h:H + h + 1] for h in heads]
        gc_r = [gct[H + h:H + h + 1, rows] for h in heads]
        g_last = [gc_r[h][:, C - 1:C] for h in heads]
        decay = [jnp.exp(jnp.where(causal, gc_c[h] - gc_r[h], NEG)) for h in heads]
        kb = [k[h] * beta_c[h] for h in heads]
        kq = [_dot_nt(jnp.concatenate([kb[h], q[h]], axis=0), k[h]) for h in heads]
        qk = [jnp.where(causal, kq[h][C:] * decay[h], 0.0) for h in heads]
        pw = [jnp.where(strict, -kq[h][:C] * decay[h], 0.0) for h in heads]
        r = pw
        pw = [_dot(pw[h], pw[h]) for h in heads]
        for _ in range(int(np.log2(C)) - 2):
            pr = [_dot(jnp.concatenate([pw[h], r[h]], axis=0), pw[h]) for h in heads]
            r = [r[h] + pw[h] + pr[h][C:] for h in heads]
            pw = [pr[h][:C] for h in heads]
        r = [r[h] + pw[h] + _dot(r[h], pw[h]) for h in heads]
        bmat = [jnp.concatenate([v[h] * beta_c[h], kb[h] * eg_c[h]], axis=-1) for h in heads]
        uw = [bmat[h] + _dot(r[h], bmat[h]) for h in heads]
        wq = [jnp.concatenate([uw[h][:, D:], q[h] * eg_c[h]], axis=0) for h in heads]
        kd = [k[h] * jnp.exp(g_last[h] - gc_c[h]) for h in heads]
        s_old = [state[h] for h in heads]
        ws = [_dot(wq[h], s_old[h]) for h in heads]
        v_new = [uw[h][:, :D] - ws[h][:C] for h in heads]
        for h in heads:
            state[h] = s_old[h] * jnp.exp(g_last[h]) + _dot_tn(kd[h], v_new[h])
        o = [ws[h][C:] + _dot(qk[h], v_new[h]) for h in heads]
        for h in heads:
            z = z_ref[rows, h * D:(h + 1) * D]
            o_ref[rows, h * D:(h + 1) * D] = _rms(o[h], gout) * (z * jax.nn.sigmoid(z))


def _gated_deltanet(proj, ba, conv_w, alog_pad, dtb_pad, g_out):
    S = proj.shape[0]
    CB = DELTA_BLOCK
    qkv_col = W_A * 3 // (3 * W_B)
    z_col = (3 * W_A + 3 * W_B) // W_B
    return pl.pallas_call(
        _delta_kernel,
        grid=(S // CB,),
        in_specs=[pl.BlockSpec((SUBLANES, 3 * W_B), lambda i: (jnp.maximum(i * (CB // SUBLANES) - 1, 0), qkv_col)),
                  pl.BlockSpec((CB, 3 * W_B), lambda i: (i, qkv_col)),
                  pl.BlockSpec((CB, W_B), lambda i: (i, z_col)),
                  pl.BlockSpec((CB, LANES), lambda i: (i, 0)),
                  pl.BlockSpec((CONV_WIDTH, 3 * W_B), lambda i: (0, 0)),
                  pl.BlockSpec((1, LANES), lambda i: (0, 0)),
                  pl.BlockSpec((1, LANES), lambda i: (0, 0)),
                  pl.BlockSpec((1, HEAD_DIM), lambda i: (0, 0))],
        out_specs=pl.BlockSpec((CB, W_B), lambda i: (i, 0)),
        out_shape=jax.ShapeDtypeStruct((S, W_B), F32),
        scratch_shapes=[pltpu.VMEM((CB + SUBLANES, 3 * W_B), F32),
                        pltpu.VMEM((CB, 3 * W_B), F32),
                        pltpu.VMEM((LANES, CB), F32),
                        pltpu.VMEM((N_HEADS_B, HEAD_DIM, HEAD_DIM), F32)],
        compiler_params=_cparams(("arbitrary",)),
        name="delta",
    )(proj, proj, proj, ba, conv_w, alog_pad, dtb_pad, g_out)


def _memkv_kernel(mem_ref, g_ref, w_ref, k_ref, v_ref):
    kv = _dot(_rms(mem_ref[...], g_ref[...]), w_ref[...])
    k_ref[...] = kv[:, :W_MEM].astype(BF16)
    v_ref[...] = kv[:, W_MEM:].astype(BF16)


def _memkv(mem, g, w_kv):
    M = mem.shape[0]
    return pl.pallas_call(
        _memkv_kernel,
        out_shape=[jax.ShapeDtypeStruct((M, W_MEM), BF16), jax.ShapeDtypeStruct((M, W_MEM), BF16)],
        compiler_params=pltpu.CompilerParams(vmem_limit_bytes=VMEM_LIMIT),
        name="memkv",
    )(mem, g, w_kv)


def _mix_kernel(attn_ref, delta_ref, x_ref, gattn_ref, wout_ref, gcross_ref, wq_ref, km_ref, vm_ref, wo_ref,
                gmoe_ref, wr_ref, br_ref, h2_ref, hn_ref, eid_ref, wts_ref, cnt_ref):
    an = _rms(attn_ref[...], gattn_ref[...])
    mix = jnp.concatenate([an.astype(BF16), delta_ref[...].astype(BF16)], axis=-1)
    h1 = x_ref[...] + jnp.dot(mix, wout_ref[...], preferred_element_type=F32)

    q = _dot(_rms(h1, gcross_ref[...]), wq_ref[...]) * (HEAD_DIM ** -0.5)
    outs = []
    for hh in range(N_MEM_HEADS):
        sl = slice(hh * HEAD_DIM, (hh + 1) * HEAD_DIM)
        s = _dot_nt(q[:, sl], km_ref[:, sl])
        e = jnp.exp(s - jnp.max(s, axis=-1, keepdims=True))
        outs.append(_dot(e, vm_ref[:, sl]) / jnp.sum(e, axis=-1, keepdims=True))
    h2 = h1 + _dot(jnp.concatenate(outs, axis=-1), wo_ref[...])
    h2_ref[...] = h2

    hn = _rms(h2, gmoe_ref[...])
    hn_ref[...] = hn
    logits = _dot_x3(hn, wr_ref) + br_ref[...]
    lane = lax.broadcasted_iota(I32, logits.shape, 1)
    gl = jnp.where(lane < N_GROUPS, logits, NEG)
    gmax = jnp.max(gl, axis=-1, keepdims=True)
    g_sel = jnp.min(jnp.where(gl == gmax, lane, LANES), axis=-1, keepdims=True)
    g_gate = 1.0 / jnp.sum(jnp.exp(gl - gmax), axis=-1, keepdims=True)
    in_group = (lane >= N_GROUPS) & ((lane - N_GROUPS) // EXPERTS_PER_GROUP == g_sel)
    el = jnp.where(in_group, logits, NEG)
    v1 = jnp.max(el, axis=-1, keepdims=True)
    i1 = jnp.min(jnp.where(in_group & (el == v1), lane, LANES), axis=-1, keepdims=True)
    in_rest = in_group & (lane != i1)
    el2 = jnp.where(in_rest, logits, NEG)
    v2 = jnp.max(el2, axis=-1, keepdims=True)
    i2 = jnp.min(jnp.where(in_rest & (el2 == v2), lane, LANES), axis=-1, keepdims=True)
    e2 = jnp.exp(v2 - v1)
    w1 = g_gate / (1.0 + e2)
    w2 = g_gate * e2 / (1.0 + e2)
    eid_ref[...] = jnp.where(lane == 0, i1 - N_GROUPS, jnp.where(lane == 1, i2 - N_GROUPS, 0))
    wts_ref[...] = jnp.where(lane == 0, w1, jnp.where(lane == 1, w2, 0.0))

    @pl.when(pl.program_id(0) == 0)
    def _():
        cnt_ref[...] = jnp.zeros_like(cnt_ref)

    picked = (lane == i1 - N_GROUPS) | (lane == i2 - N_GROUPS)
    cnt_ref[...] = cnt_ref[...] + jnp.sum(jnp.where(picked, 1.0, 0.0), axis=0, keepdims=True)


def _mix(attn, delta, x, g_attn, w_out, g_cross, w_q, k_mem, v_mem, w_o, g_moe, w_router, b_router, tm=512):
    S, D = x.shape
    row = lambda w: pl.BlockSpec((tm, w), lambda i: (i, 0))
    full = lambda a: pl.BlockSpec(a.shape, lambda i: (0, 0), pipeline_mode=pl.Buffered(1))
    return pl.pallas_call(
        _mix_kernel,
        grid=(S // tm,),
        in_specs=[row(W_A), row(W_B), row(D), full(g_attn), full(w_out), full(g_cross), full(w_q),
                  full(k_mem), full(v_mem), full(w_o), full(g_moe), full(w_router), full(b_router)],
        out_specs=[row(D), row(D), row(LANES), row(LANES), pl.BlockSpec((SUBLANES, LANES), lambda i: (0, 0))],
        out_shape=[jax.ShapeDtypeStruct((S, D), F32), jax.ShapeDtypeStruct((S, D), F32),
                   jax.ShapeDtypeStruct((S, LANES), I32), jax.ShapeDtypeStruct((S, LANES), F32),
                   jax.ShapeDtypeStruct((SUBLANES, LANES), F32)],
        compiler_params=_cparams(("arbitrary",)),
        name="mix",
    )(attn, delta, x, g_attn, w_out, g_cross, w_q, k_mem, v_mem, w_o, g_moe, w_router, b_router)


def _rank_kernel(eid_ref, cnt_ref, dest_ref, meta_ref, tab_ref, carry, pstart):
    i = pl.program_id(0)
    tm = eid_ref.shape[0]
    lane = lax.broadcasted_iota(I32, (tm, LANES), 1)
    e = eid_ref[...]
    e1 = e[:, 0:1]
    e2 = e[:, 1:2]
    onehot = jnp.where((lane == e1) | (lane == e2), 1.0, 0.0).astype(F32)

    @pl.when(i == 0)
    def _():
        cnt = cnt_ref[...]
        lane8 = lax.broadcasted_iota(I32, cnt.shape, 1)
        is_expert = lane8 < N_EXPERTS

        def prefix_sum(x):
            shift = 1
            while shift < LANES:
                x = x + jnp.where(lane8 >= shift, pltpu.roll(x, shift, 1), 0.0)
                shift *= 2
            return x

        padded = jnp.floor((cnt + (MOE_BLOCK - 1)) * (1.0 / MOE_BLOCK)) * MOE_BLOCK
        pend = prefix_sum(padded)
        pstart[...] = pend - padded
        carry[...] = jnp.zeros_like(carry)

        nb = meta_ref.shape[0]
        blane = lax.broadcasted_iota(I32, (nb, LANES), 1)
        brow = (lax.broadcasted_iota(I32, (nb, LANES), 0) * MOE_BLOCK).astype(F32)
        owner = jnp.sum(jnp.where((blane < N_EXPERTS) & (brow >= pend[0:1, :]), 1.0, 0.0), axis=-1, keepdims=True)
        owner = jnp.minimum(owner, N_EXPERTS - 1.0)
        seg_end = jnp.sum(jnp.where(blane.astype(F32) == owner, (pend - padded + cnt)[0:1, :], 0.0),
                          axis=-1, keepdims=True)
        valid = jnp.clip(seg_end - brow, 0.0, MOE_BLOCK * 1.0)
        meta_ref[...] = jnp.where(blane == 0, owner, jnp.where(blane == 1, valid, 0.0)).astype(I32)

        present = is_expert & (cnt > 0.0)
        order = prefix_sum(jnp.where(present, 1.0, 0.0)) - 1.0
        parity = order - 2.0 * jnp.floor(order * 0.5)
        nxt = jnp.where(present, lane8.astype(F32), N_EXPERTS * 1.0)
        nxt = jnp.where(lane8 < LANES - 1, pltpu.roll(nxt, LANES - 1, 1), N_EXPERTS * 1.0)
        shift = 1
        while shift < LANES:
            nxt = jnp.minimum(nxt, jnp.where(lane8 < LANES - shift, pltpu.roll(nxt, LANES - shift, 1), N_EXPERTS * 1.0))
            shift *= 2
        nxt = jnp.where(nxt >= N_EXPERTS, -1.0, nxt)
        sub8 = lax.broadcasted_iota(I32, cnt.shape, 0)
        tab_ref[...] = jnp.where(sub8 == 0, parity, nxt).astype(I32)

    r = lax.broadcasted_iota(I32, (tm, tm), 0)
    c = lax.broadcasted_iota(I32, (tm, tm), 1)
    before = jnp.where(r > c, 1.0, 0.0).astype(BF16)
    pos = jnp.dot(before, onehot.astype(BF16), preferred_element_type=F32) + carry[0:1, :] + pstart[0:1, :]
    d1 = jnp.sum(jnp.where(lane == e1, pos, 0.0), axis=-1, keepdims=True)
    d2 = jnp.sum(jnp.where(lane == e2, pos, 0.0), axis=-1, keepdims=True)
    dest = jnp.where(lane == 0, d1, jnp.where(lane == 1, d2, 0.0))
    dest_ref[...] = dest.T[:SUBLANES].astype(I32)
    carry[...] = carry[...] + jnp.sum(onehot, axis=0, keepdims=True)


def _rank(eid, cnt, n_blocks, tm=256):
    S = eid.shape[0]
    const = lambda shape: pl.BlockSpec(shape, lambda i: (0, 0))
    return pl.pallas_call(
        _rank_kernel,
        grid=(S // tm,),
        in_specs=[pl.BlockSpec((tm, LANES), lambda i: (i, 0)), const((SUBLANES, LANES))],
        out_specs=[pl.BlockSpec((SUBLANES, tm), lambda i: (0, i)), const((n_blocks, LANES)), const((SUBLANES, LANES))],
        out_shape=[jax.ShapeDtypeStruct((SUBLANES, S), I32), jax.ShapeDtypeStruct((n_blocks, LANES), I32),
                   jax.ShapeDtypeStruct((SUBLANES, LANES), I32)],
        scratch_shapes=[pltpu.VMEM((SUBLANES, LANES), F32), pltpu.VMEM((SUBLANES, LANES), F32)],
        compiler_params=_cparams(("arbitrary",)),
        name="rank",
    )(eid, cnt)


def _source_kernel(dest_ref, src_ref, *, n_tok):
    def clear(p, carry):
        src_ref[p] = 0
        return carry

    lax.fori_loop(0, src_ref.shape[0], clear, 0, unroll=ROW_DMA_UNROLL)

    def place(t, carry):
        for k in range(TOP_K):
            src_ref[dest_ref[k * n_tok + t]] = t
        return carry

    lax.fori_loop(0, n_tok, place, 0, unroll=ROW_DMA_UNROLL)


def _source_rows(dest_flat, n_tok, n_rows):
    return pl.pallas_call(
        functools.partial(_source_kernel, n_tok=n_tok),
        grid_spec=pltpu.PrefetchScalarGridSpec(
            num_scalar_prefetch=1,
            grid=(1,),
            in_specs=[],
            out_specs=pl.BlockSpec(memory_space=pltpu.SMEM)),
        out_shape=jax.ShapeDtypeStruct((n_rows,), I32),
        compiler_params=_cparams(("arbitrary",)),
        name="source",
    )(dest_flat)


def _expert_kernel(bexp_ref, nval_ref, slot_ref, next_ref, src_ref, hn_ref, wg_ref, wu_ref, wd_ref, y_ref,
                   xbuf, wg_f, wu_f, wd_f, wg_s, wu_s, wd_s, sem, xsem):
    b = pl.program_id(0)
    nb = pl.num_programs(0)
    e = bexp_ref[b]
    e_prev = bexp_ref[jnp.maximum(b - 1, 0)]
    nval = nval_ref[b]

    def gather_block(blk, xslot):
        def issue(r, carry):
            tok = src_ref[blk * MOE_BLOCK + r]
            pltpu.make_async_copy(hn_ref.at[pl.ds(tok, 1), :], xbuf.at[xslot, pl.ds(r, 1), :], xsem.at[xslot]).start()
            return carry

        lax.fori_loop(0, MOE_BLOCK, issue, 0, unroll=ROW_DMA_UNROLL)

    @pl.when(jnp.logical_and(b == 0, nval > 0))
    def _():
        gather_block(0, 0)

    @pl.when(jnp.logical_and(b + 1 < nb, nval_ref[jnp.minimum(b + 1, nb - 1)] > 0))
    def _():
        gather_block(b + 1, (b + 1) % 2)

    def weight_copies(ex, slot):
        copies = []
        for w, (src, dst) in enumerate(((wg_ref, wg_f), (wu_ref, wu_f), (wd_ref, wd_f))):
            rows = src.shape[1] // EXPERT_DMA_CHUNKS
            for c in range(EXPERT_DMA_CHUNKS):
                sl = pl.ds(c * rows, rows)
                copies.append(pltpu.make_async_copy(src.at[ex, sl, :], dst.at[slot, sl, :], sem.at[slot, w]))
        return copies

    @pl.when(jnp.logical_and(b == 0, nval > 0))
    def _():
        for c in weight_copies(e, slot_ref[e]):
            c.start()

    @pl.when(jnp.logical_and(nval > 0, jnp.logical_or(b == 0, e != e_prev)))
    def _():
        slot = slot_ref[e]
        for c in weight_copies(e, slot):
            c.wait()
        nxt = next_ref[e]

        @pl.when(nxt >= 0)
        def _():
            for c in weight_copies(nxt, 1 - slot):
                c.start()

        wg_s[...] = wg_f[slot].astype(BF16)
        wu_s[...] = wu_f[slot].astype(BF16)
        wd_s[...] = wd_f[slot].astype(BF16)

    @pl.when(nval == 0)
    def _():
        y_ref[...] = jnp.zeros_like(y_ref)

    @pl.when(nval > 0)
    def _():
        xslot = b % 2
        pltpu.make_async_copy(hn_ref.at[pl.ds(0, MOE_BLOCK), :], xbuf.at[xslot], xsem.at[xslot]).wait()
        x = xbuf[xslot].astype(BF16)
        gate = jnp.dot(x, wg_s[...], preferred_element_type=F32)
        up = jnp.dot(x, wu_s[...], preferred_element_type=F32)
        hid = gate * jax.nn.sigmoid(gate) * up
        y_ref[...] = jnp.dot(hid.astype(BF16), wd_s[...], preferred_element_type=F32)


def _experts(block_expert, block_valid, expert_slot, expert_next, src_rows, hn, w_gate, w_up, w_down):
    D = hn.shape[1]
    P = src_rows.shape[0]
    DE = w_gate.shape[-1]
    nb = P // MOE_BLOCK
    hbm = pl.BlockSpec(memory_space=pl.ANY)
    return pl.pallas_call(
        _expert_kernel,
        grid_spec=pltpu.PrefetchScalarGridSpec(
            num_scalar_prefetch=5,
            grid=(nb,),
            in_specs=[hbm, hbm, hbm, hbm],
            out_specs=pl.BlockSpec((MOE_BLOCK, D), lambda b, *_: (b, 0)),
            scratch_shapes=[pltpu.VMEM((2, MOE_BLOCK, D), F32),
                            pltpu.VMEM((2, D, DE), F32), pltpu.VMEM((2, D, DE), F32), pltpu.VMEM((2, DE, D), F32),
                            pltpu.VMEM((D, DE), BF16), pltpu.VMEM((D, DE), BF16), pltpu.VMEM((DE, D), BF16),
                            pltpu.SemaphoreType.DMA((2, 3)), pltpu.SemaphoreType.DMA((2,))]),
        out_shape=jax.ShapeDtypeStruct((P, D), F32),
        compiler_params=_cparams(("arbitrary",)),
        name="experts",
    )(block_expert, block_valid, expert_slot, expert_next, src_rows, hn, w_gate, w_up, w_down)


def _combine_kernel(dest_ref, h2_ref, wts_ref, g_ref, yb_ref, o_ref, buf, sem):
    i = pl.program_id(0)
    n = pl.num_programs(0)
    tm = h2_ref.shape[0]
    n_tok = n * tm

    def gather_block(blk, slot):
        def issue(r, carry):
            for k in range(TOP_K):
                d = dest_ref[k * n_tok + blk * tm + r]
                pltpu.make_async_copy(yb_ref.at[pl.ds(d, 1), :], buf.at[slot, k, pl.ds(r, 1), :], sem.at[slot]).start()
            return carry

        lax.fori_loop(0, tm, issue, 0, unroll=ROW_DMA_UNROLL)

    @pl.when(i == 0)
    def _():
        gather_block(0, 0)

    @pl.when(i + 1 < n)
    def _():
        gather_block(i + 1, (i + 1) % 2)

    slot = i % 2
    for k in range(TOP_K):
        pltpu.make_async_copy(yb_ref.at[pl.ds(0, tm), :], buf.at[slot, k], sem.at[slot]).wait()
    wts = wts_ref[...]
    h = h2_ref[...] + wts[:, 0:1] * buf[slot, 0] + wts[:, 1:2] * buf[slot, 1]
    o_ref[...] = _rms(h, g_ref[...])


def _combine(dest_flat, h2, wts, g_final, yb, tm=256):
    S, D = h2.shape
    return pl.pallas_call(
        _combine_kernel,
        grid_spec=pltpu.PrefetchScalarGridSpec(
            num_scalar_prefetch=1,
            grid=(S // tm,),
            in_specs=[pl.BlockSpec((tm, D), lambda i, d: (i, 0)),
                      pl.BlockSpec((tm, LANES), lambda i, d: (i, 0)),
                      pl.BlockSpec((1, D), lambda i, d: (0, 0)),
                      pl.BlockSpec(memory_space=pl.ANY)],
            out_specs=pl.BlockSpec((tm, D), lambda i, d: (i, 0)),
            scratch_shapes=[pltpu.VMEM((2, TOP_K, tm, D), F32), pltpu.SemaphoreType.DMA((2,))]),
        out_shape=jax.ShapeDtypeStruct((S, D), F32),
        compiler_params=_cparams(("arbitrary",)),
        name="combine",
    )(dest_flat, h2, wts, g_final, yb)


def _lane_pad(v, offset):
    return jnp.zeros((1, LANES), F32).at[0, offset:offset + v.shape[0]].set(v.astype(F32))


def _layer(h, mem, g_mix, w_in, conv_w, a_log, dt_bias, g_delta_out, g_attn_out, w_out, g_cross, g_mem,
           w_q_mem, w_kv_mem, w_o_mem, g_moe, w_group, b_group, w_expert, b_expert, w_gate, w_up, w_down,
           g_final):
    S, D = h.shape
    H = N_HEADS_B
    w_main, w_small = _cast_w_in(w_in)
    proj, ba = _inproj(h, g_mix[None], w_main, w_small)

    slopes = jnp.asarray(2.0 ** (-8.0 * np.arange(1, N_HEADS_A + 1) / N_HEADS_A), dtype=F32)
    attn = _dilated_attention(proj, slopes)
    delta = _gated_deltanet(proj, ba, conv_w, _lane_pad(a_log, H), _lane_pad(dt_bias, H), g_delta_out[None])

    k_mem, v_mem = _memkv(mem, g_mem[None], w_kv_mem.astype(BF16))
    w_router = _split_hi_lo(jnp.pad(jnp.concatenate([w_group, w_expert], axis=1),
                                    ((0, 0), (0, LANES - N_GROUPS - N_EXPERTS))))
    b_router = _lane_pad(jnp.concatenate([b_group, b_expert]), 0)
    h2, hn, eid, wts, cnt = _mix(attn, delta, h, g_attn_out[None], w_out.astype(BF16), g_cross[None],
                                 w_q_mem.astype(BF16), k_mem, v_mem, w_o_mem.astype(BF16), g_moe[None],
                                 w_router, b_router)

    n_rows = S * TOP_K + N_EXPERTS * MOE_BLOCK
    dest, meta, tab = _rank(eid, cnt, n_rows // MOE_BLOCK)
    dest_flat = dest[:TOP_K].reshape(TOP_K * S)
    src_rows = _source_rows(dest_flat, S, n_rows)
    yb = _experts(meta[:, 0], meta[:, 1], tab[0, :N_EXPERTS], tab[1, :N_EXPERTS], src_rows, hn,
                  w_gate, w_up, w_down)
    return _combine(dest_flat, h2, wts, g_final[None], yb)


def kernel(x, mem, g_mix, w_in, conv_w, a_log, dt_bias, g_delta_out, g_attn_out, w_out, g_cross, g_mem, w_q_mem,
           w_kv_mem, w_o_mem, g_moe, w_group, b_group, w_expert, b_expert, w_gate, w_up, w_down, g_final):
    assert x.shape[0] == 1 and mem.shape[0] == 1 and g_mix.shape[0] == 1
    out = _layer(x[0].astype(F32), mem[0].astype(F32), g_mix[0], w_in[0], conv_w[0], a_log[0], dt_bias[0],
                 g_delta_out[0], g_attn_out[0], w_out[0], g_cross[0], g_mem[0], w_q_mem[0], w_kv_mem[0],
                 w_o_mem[0], g_moe[0], w_group[0], b_group[0], w_expert[0], b_expert[0], w_gate[0], w_up[0],
                 w_down[0], g_final)
    return out[None].astype(x.dtype)
```

```python
import functools

import numpy as np
import jax
import jax.numpy as jnp
from jax import lax
from jax.experimental import pallas as pl
from jax.experimental.pallas import tpu as pltpu

F32 = jnp.float32
BF16 = jnp.bfloat16
I32 = jnp.int32
HIGHEST = lax.Precision.HIGHEST

EPS = 1e-6
HEAD_DIM = 128
LANES = 128
SUBLANES = 8
NEG = -1e30

N_HEADS_A = 8
N_HEADS_B = 8
W_A = N_HEADS_A * HEAD_DIM
W_B = N_HEADS_B * HEAD_DIM
W_MAIN = 3 * W_A + 4 * W_B
DILATED_PATTERNS = ((128, 1), (512, 4), (2048, 16))
ATTN_BLOCK = 128
ATTN_TILE = 2048
CONV_WIDTH = 4
DELTA_CHUNK = 64
DELTA_BLOCK = 256
N_MEM_HEADS = 4
W_MEM = N_MEM_HEADS * HEAD_DIM
N_GROUPS = 8
EXPERTS_PER_GROUP = 8
N_EXPERTS = 64
TOP_K = 2
MOE_BLOCK = 128
ROW_DMA_UNROLL = 16

VMEM_LIMIT = 56 * 1024 * 1024


def _cparams(sem):
    return pltpu.CompilerParams(dimension_semantics=sem, vmem_limit_bytes=VMEM_LIMIT)


def _rms(x, g):
    return x * lax.rsqrt(jnp.mean(x * x, axis=-1, keepdims=True) + EPS) * g


def _dot(a, b):
    return jnp.dot(a.astype(BF16), b.astype(BF16), preferred_element_type=F32)


def _dot_nt(a, b):
    return lax.dot_general(a.astype(BF16), b.astype(BF16), (((1,), (1,)), ((), ())), preferred_element_type=F32)


def _dot_tn(a, b):
    return lax.dot_general(a.astype(BF16), b.astype(BF16), (((0,), (0,)), ((), ())), preferred_element_type=F32)


def _split_hi_lo(w):
    hi = w.astype(BF16)
    lo = (w - hi.astype(F32)).astype(BF16)
    return jnp.concatenate([hi, lo], axis=1)


def _dot_x3(x, w_hl_ref):
    x_hi = x.astype(BF16)
    x_lo = (x - x_hi.astype(F32)).astype(BF16)
    r = jnp.dot(x_hi, w_hl_ref[...], preferred_element_type=F32)
    return r[:, :LANES] + r[:, LANES:] + jnp.dot(x_lo, w_hl_ref[:, :LANES], preferred_element_type=F32)


def _castw_kernel(w_ref, ws_ref, o_ref, os_ref):
    o_ref[...] = w_ref[...].astype(BF16)

    @pl.when(pl.program_id(0) == 0)
    def _():
        w = ws_ref[...]
        hi = w.astype(BF16)
        os_ref[...] = jnp.zeros_like(os_ref)
        os_ref[0:w.shape[0], :] = hi
        os_ref[LANES:LANES + w.shape[0], :] = (w - hi.astype(F32)).astype(BF16)


def _cast_w_in(w_t, tn=1024):
    D = w_t.shape[1]
    n_gate = 2 * N_HEADS_B
    assert w_t.shape[0] == W_MAIN + n_gate and W_MAIN % tn == 0 and W_MAIN % n_gate == 0
    return pl.pallas_call(
        _castw_kernel,
        grid=(W_MAIN // tn,),
        in_specs=[pl.BlockSpec((tn, D), lambda j: (j, 0)),
                  pl.BlockSpec((n_gate, D), lambda j: (W_MAIN // n_gate, 0))],
        out_specs=[pl.BlockSpec((tn, D), lambda j: (j, 0)),
                   pl.BlockSpec((2 * LANES, D), lambda j: (0, 0))],
        out_shape=[jax.ShapeDtypeStruct((W_MAIN, D), BF16), jax.ShapeDtypeStruct((2 * LANES, D), BF16)],
        compiler_params=_cparams(("arbitrary",)),
        name="castw",
    )(w_t, w_t)


def _inproj_kernel(x_ref, g_ref, w_ref, ws_ref, o_ref, os_ref, u_scr):
    @pl.when(pl.program_id(1) == 0)
    def _():
        u = _rms(x_ref[...], g_ref[...])
        u_hi = u.astype(BF16)
        u_scr[...] = u_hi
        u_lo = (u - u_hi.astype(F32)).astype(BF16)
        r = _dot_nt(u_hi, ws_ref[...])
        os_ref[...] = r[:, :LANES] + r[:, LANES:] + _dot_nt(u_lo, ws_ref[0:LANES, :])

    o_ref[...] = _dot_nt(u_scr[...], w_ref[...])


def _inproj(x, g, w_main, w_small, tm=1024, tn=1024):
    S, D = x.shape
    N = w_main.shape[0]
    return pl.pallas_call(
        _inproj_kernel,
        grid=(S // tm, N // tn),
        in_specs=[pl.BlockSpec((tm, D), lambda i, j: (i, 0)),
                  pl.BlockSpec((1, D), lambda i, j: (0, 0)),
                  pl.BlockSpec((tn, D), lambda i, j: (j, 0)),
                  pl.BlockSpec((2 * LANES, D), lambda i, j: (0, 0))],
        out_specs=[pl.BlockSpec((tm, tn), lambda i, j: (i, j)),
                   pl.BlockSpec((tm, LANES), lambda i, j: (i, 0))],
        out_shape=[jax.ShapeDtypeStruct((S, N), F32), jax.ShapeDtypeStruct((S, LANES), F32)],
        scratch_shapes=[pltpu.VMEM((tm, D), BF16)],
        compiler_params=_cparams(("parallel", "arbitrary")),
        name="inproj",
    )(x, g, w_main, w_small)


def _attn_kernel(slope_ref, q_ref, kp_ref, kc_ref, vp_ref, vc_ref, o_ref, kk, vv, o_scr, l_scr):
    i = pl.program_id(0)
    h = pl.program_id(1)
    T = ATTN_TILE
    B = ATTN_BLOCK
    slope = slope_ref[h]
    kk[0:T, :] = kp_ref[...]
    kk[T:2 * T, :] = kc_ref[...]
    vv[0:T, :] = vp_ref[...]
    vv[T:2 * T, :] = vc_ref[...]
    qi = lax.broadcasted_iota(I32, (B, 2 * B), 0)
    ki = lax.broadcasted_iota(I32, (B, 2 * B), 1)
    dist = qi + B - ki
    band = (dist >= 0) & (dist <= B)
    band_first = band & (ki >= B)
    distf = dist.astype(F32)
    scale = HEAD_DIM ** -0.5

    for p, (window, d) in enumerate(DILATED_PATTERNS):
        assert window // d == B and T % (B * d) == 0
        bias = jnp.where(band, distf * (-slope * d), NEG)
        bias_first = jnp.where(band_first, distf * (-slope * d), NEG)

        def body(b, carry, p=p, d=d, bias=bias, bias_first=bias_first):
            r = b % d
            j = b // d
            qs = r + B * d * j
            ks = T - B * d + qs
            q = q_ref[pl.ds(qs, B, stride=d), :] * scale
            k = kk[pl.ds(ks, 2 * B, stride=d), :]
            v = vv[pl.ds(ks, 2 * B, stride=d), :]
            s = _dot_nt(q, k)
            first = jnp.logical_and(i == 0, j == 0)
            s = s + jnp.where(first, bias_first, bias)
            m = jnp.max(s, axis=-1, keepdims=True)
            e = jnp.exp(s - m)
            l = jnp.sum(e, axis=-1, keepdims=True)
            o = _dot(e, v) / l
            lse = m + jnp.log(l)
            o_scr[p, pl.ds(qs, B, stride=d), :] = o
            l_scr[p, pl.ds(qs, B, stride=d), :] = jnp.broadcast_to(lse, (B, HEAD_DIM))
            return carry

        lax.fori_loop(0, T // B, body, 0, unroll=8)

    l0, l1, l2 = l_scr[0], l_scr[1], l_scr[2]
    m = jnp.maximum(jnp.maximum(l0, l1), l2)
    w0, w1, w2 = jnp.exp(l0 - m), jnp.exp(l1 - m), jnp.exp(l2 - m)
    o_ref[...] = (w0 * o_scr[0] + w1 * o_scr[1] + w2 * o_scr[2]) / (w0 + w1 + w2)


def _dilated_attention(proj, slopes):
    S = proj.shape[0]
    T = ATTN_TILE
    H = N_HEADS_A
    blk = (T, HEAD_DIM)
    return pl.pallas_call(
        _attn_kernel,
        grid_spec=pltpu.PrefetchScalarGridSpec(
            num_scalar_prefetch=1,
            grid=(S // T, H),
            in_specs=[pl.BlockSpec(blk, lambda i, h, s: (i, h)),
                      pl.BlockSpec(blk, lambda i, h, s: (jnp.maximum(i - 1, 0), H + h)),
                      pl.BlockSpec(blk, lambda i, h, s: (i, H + h)),
                      pl.BlockSpec(blk, lambda i, h, s: (jnp.maximum(i - 1, 0), 2 * H + h)),
                      pl.BlockSpec(blk, lambda i, h, s: (i, 2 * H + h))],
            out_specs=pl.BlockSpec(blk, lambda i, h, s: (i, h)),
            scratch_shapes=[pltpu.VMEM((2 * T, HEAD_DIM), F32), pltpu.VMEM((2 * T, HEAD_DIM), F32),
                            pltpu.VMEM((3, T, HEAD_DIM), F32), pltpu.VMEM((3, T, HEAD_DIM), F32)]),
        out_shape=jax.ShapeDtypeStruct((S, W_A), F32),
        compiler_params=_cparams(("parallel", "parallel")),
        name="attn",
    )(slopes, proj, proj, proj, proj, proj)


def _softplus(x):
    return jnp.maximum(x, 0.0) + jnp.log1p(jnp.exp(-jnp.abs(x)))


def _delta_kernel(hist_ref, qkv_ref, z_ref, ba_ref, cw_ref, alog_ref, dtb_ref, gout_ref, o_ref,
                  ext, act, gct, state):
    i = pl.program_id(0)
    CB = DELTA_BLOCK
    C = DELTA_CHUNK
    H = N_HEADS_B
    D = HEAD_DIM

    @pl.when(i == 0)
    def _():
        state[...] = jnp.zeros_like(state)
        ext[0:SUBLANES, :] = jnp.zeros((SUBLANES, 3 * W_B), F32)

    @pl.when(i > 0)
    def _():
        ext[0:SUBLANES, :] = hist_ref[...]

    ext[SUBLANES:SUBLANES + CB, :] = qkv_ref[...]

    for c in range(3 * H):
        sl = slice(c * D, (c + 1) * D)
        acc = cw_ref[CONV_WIDTH - 1:CONV_WIDTH, sl] * ext[SUBLANES:SUBLANES + CB, sl]
        for t in range(1, CONV_WIDTH):
            acc = acc + cw_ref[CONV_WIDTH - 1 - t:CONV_WIDTH - t, sl] * ext[SUBLANES - t:SUBLANES - t + CB, sl]
        a = acc * jax.nn.sigmoid(acc)
        if c < 2 * H:
            a = a * lax.rsqrt(jnp.sum(a * a, axis=-1, keepdims=True) + EPS)
        if c < H:
            a = a * (D ** -0.5)
        act[:, sl] = a

    ba = ba_ref[...]
    beta = jax.nn.sigmoid(ba)
    g = -jnp.exp(alog_ref[...]) * _softplus(ba + dtb_ref[...])
    row = lax.broadcasted_iota(I32, (CB, CB), 0)
    col = lax.broadcasted_iota(I32, (CB, CB), 1)
    chunk_tri = jnp.where((row // C == col // C) & (row >= col), 1.0, 0.0).astype(F32)
    gc = jnp.dot(chunk_tri, g, precision=HIGHEST, preferred_element_type=F32)
    gct[...] = gc.T
    eg = jnp.exp(gc)

    ri = lax.broadcasted_iota(I32, (C, C), 0)
    ci = lax.broadcasted_iota(I32, (C, C), 1)
    causal = ri >= ci
    strict = ri > ci
    gout = gout_ref[...]

    heads = range(H)
    for n in range(CB // C):
        rows = slice(n * C, (n + 1) * C)
        q = [act[rows, h * D:(h + 1) * D] for h in heads]
        k = [act[rows, (H + h) * D:(H + h + 1) * D] for h in heads]
        v = [act[rows, (2 * H + h) * D:(2 * H + h + 1) * D] for h in heads]
        beta_c = [beta[rows, h:h + 1] for h in heads]
        gc_c = [gc[rows, H + h:H + h + 1] for h in heads]
        eg_c = [eg[rows, H + h:H + h + 1] for h in heads]
        gc_r = [gct[H + h:H + h + 1, rows] for h in heads]
        g_last = [gc_r[h][:, C - 1:C] for h in heads]
        decay = [jnp.exp(jnp.where(causal, gc_c[h] - gc_r[h], NEG)) for h in heads]
        kb = [k[h] * beta_c[h] for h in heads]
        kq = [_dot_nt(jnp.concatenate([kb[h], q[h]], axis=0), k[h]) for h in heads]
        qk = [jnp.where(causal, kq[h][C:] * decay[h], 0.0) for h in heads]
        pw = [jnp.where(strict, -kq[h][:C] * decay[h], 0.0) for h in heads]
        r = pw
        pw = [_dot(pw[h], pw[h]) for h in heads]
        for _ in range(int(np.log2(C)) - 2):
            pr = [_dot(jnp.concatenate([pw[h], r[h]], axis=0), pw[h]) for h in heads]
            r = [r[h] + pw[h] + pr[h][C:] for h in heads]
            pw = [pr[h][:C] for h in heads]
        r = [r[h] + pw[h] + _dot(r[h], pw[h]) for h in heads]
        bmat = [jnp.concatenate([v[h] * beta_c[h], kb[h] * eg_c[h]], axis=-1) for h in heads]
        uw = [bmat[h] + _dot(r[h], bmat[h]) for h in heads]
        wq = [jnp.concatenate([uw[h][:, D:], q[h] * eg_c[h]], axis=0) for h in heads]
        kd = [k[h] * jnp.exp(g_last[h] - gc_c[h]) for h in heads]
        s_old = [state[h] for h in heads]
        ws = [_dot(wq[h], s_old[h]) for h in heads]
        v_new = [uw[h][:, :D] - ws[h][:C] for h in heads]
        for h in heads:
            state[h] = s_old[h] * jnp.exp(g_last[h]) + _dot_tn(kd[h], v_new[h])
        o = [ws[h][C:] + _dot(qk[h], v_new[h]) for h in heads]
        for h in heads:
            z = z_ref[rows, h * D:(h + 1) * D]
            o_ref[rows, h * D:(h + 1) * D] = _rms(o[h], gout) * (z * jax.nn.sigmoid(z))


def _gated_deltanet(proj, ba, conv_w, alog_pad, dtb_pad, g_out):
    S = proj.shape[0]
    CB = DELTA_BLOCK
    qkv_col = W_A * 3 // (3 * W_B)
    z_col = (3 * W_A + 3 * W_B) // W_B
    return pl.pallas_call(
        _delta_kernel,
        grid=(S // CB,),
        in_specs=[pl.BlockSpec((SUBLANES, 3 * W_B), lambda i: (jnp.maximum(i * (CB // SUBLANES) - 1, 0), qkv_col)),
                  pl.BlockSpec((CB, 3 * W_B), lambda i: (i, qkv_col)),
                  pl.BlockSpec((CB, W_B), lambda i: (i, z_col)),
                  pl.BlockSpec((CB, LANES), lambda i: (i, 0)),
                  pl.BlockSpec((CONV_WIDTH, 3 * W_B), lambda i: (0, 0)),
                  pl.BlockSpec((1, LANES), lambda i: (0, 0)),
                  pl.BlockSpec((1, LANES), lambda i: (0, 0)),
                  pl.BlockSpec((1, HEAD_DIM), lambda i: (0, 0))],
        out_specs=pl.BlockSpec((CB, W_B), lambda i: (i, 0)),
        out_shape=jax.ShapeDtypeStruct((S, W_B), F32),
        scratch_shapes=[pltpu.VMEM((CB + SUBLANES, 3 * W_B), F32),
                        pltpu.VMEM((CB, 3 * W_B), F32),
                        pltpu.VMEM((LANES, CB), F32),
                        pltpu.VMEM((N_HEADS_B, HEAD_DIM, HEAD_DIM), F32)],
        compiler_params=_cparams(("arbitrary",)),
        name="delta",
    )(proj, proj, proj, ba, conv_w, alog_pad, dtb_pad, g_out)


def _memkv_kernel(mem_ref, g_ref, w_ref, k_ref, v_ref):
    kv = _dot(_rms(mem_ref[...], g_ref[...]), w_ref[...])
    k_ref[...] = kv[:, :W_MEM].astype(BF16)
    v_ref[...] = kv[:, W_MEM:].astype(BF16)


def _memkv(mem, g, w_kv):
    M = mem.shape[0]
    return pl.pallas_call(
        _memkv_kernel,
        out_shape=[jax.ShapeDtypeStruct((M, W_MEM), BF16), jax.ShapeDtypeStruct((M, W_MEM), BF16)],
        compiler_params=pltpu.CompilerParams(vmem_limit_bytes=VMEM_LIMIT),
        name="memkv",
    )(mem, g, w_kv)


def _mix_kernel(attn_ref, delta_ref, x_ref, gattn_ref, wout_ref, gcross_ref, wq_ref, km_ref, vm_ref, wo_ref,
                gmoe_ref, wr_ref, br_ref, h2_ref, hn_ref, eid_ref, wts_ref, cnt_ref):
    an = _rms(attn_ref[...], gattn_ref[...])
    mix = jnp.concatenate([an.astype(BF16), delta_ref[...].astype(BF16)], axis=-1)
    h1 = x_ref[...] + jnp.dot(mix, wout_ref[...], preferred_element_type=F32)

    q = _dot(_rms(h1, gcross_ref[...]), wq_ref[...]) * (HEAD_DIM ** -0.5)
    outs = []
    for hh in range(N_MEM_HEADS):
        sl = slice(hh * HEAD_DIM, (hh + 1) * HEAD_DIM)
        s = _dot_nt(q[:, sl], km_ref[:, sl])
        e = jnp.exp(s - jnp.max(s, axis=-1, keepdims=True))
        outs.append(_dot(e, vm_ref[:, sl]) / jnp.sum(e, axis=-1, keepdims=True))
    h2 = h1 + _dot(jnp.concatenate(outs, axis=-1), wo_ref[...])
    h2_ref[...] = h2

    hn = _rms(h2, gmoe_ref[...])
    hn_ref[...] = hn
    logits = _dot_x3(hn, wr_ref) + br_ref[...]
    lane = lax.broadcasted_iota(I32, logits.shape, 1)
    gl = jnp.where(lane < N_GROUPS, logits, NEG)
    gmax = jnp.max(gl, axis=-1, keepdims=True)
    g_sel = jnp.min(jnp.where(gl == gmax, lane, LANES), axis=-1, keepdims=True)
    g_gate = 1.0 / jnp.sum(jnp.exp(gl - gmax), axis=-1, keepdims=True)
    in_group = (lane >= N_GROUPS) & ((lane - N_GROUPS) // EXPERTS_PER_GROUP == g_sel)
    el = jnp.where(in_group, logits, NEG)
    v1 = jnp.max(el, axis=-1, keepdims=True)
    i1 = jnp.min(jnp.where(in_group & (el == v1), lane, LANES), axis=-1, keepdims=True)
    in_rest = in_group & (lane != i1)
    el2 = jnp.where(in_rest, logits, NEG)
    v2 = jnp.max(el2, axis=-1, keepdims=True)
    i2 = jnp.min(jnp.where(in_rest & (el2 == v2), lane, LANES), axis=-1, keepdims=True)
    e2 = jnp.exp(v2 - v1)
    w1 = g_gate / (1.0 + e2)
    w2 = g_gate * e2 / (1.0 + e2)
    eid_ref[...] = jnp.where(lane == 0, i1 - N_GROUPS, jnp.where(lane == 1, i2 - N_GROUPS, 0))
    wts_ref[...] = jnp.where(lane == 0, w1, jnp.where(lane == 1, w2, 0.0))

    @pl.when(pl.program_id(0) == 0)
    def _():
        cnt_ref[...] = jnp.zeros_like(cnt_ref)

    picked = (lane == i1 - N_GROUPS) | (lane == i2 - N_GROUPS)
    cnt_ref[...] = cnt_ref[...] + jnp.sum(jnp.where(picked, 1.0, 0.0), axis=0, keepdims=True)


def _mix(attn, delta, x, g_attn, w_out, g_cross, w_q, k_mem, v_mem, w_o, g_moe, w_router, b_router, tm=512):
    S, D = x.shape
    row = lambda w: pl.BlockSpec((tm, w), lambda i: (i, 0))
    full = lambda a: pl.BlockSpec(a.shape, lambda i: (0, 0), pipeline_mode=pl.Buffered(1))
    return pl.pallas_call(
        _mix_kernel,
        grid=(S // tm,),
        in_specs=[row(W_A), row(W_B), row(D), full(g_attn), full(w_out), full(g_cross), full(w_q),
                  full(k_mem), full(v_mem), full(w_o), full(g_moe), full(w_router), full(b_router)],
        out_specs=[row(D), row(D), row(LANES), row(LANES), pl.BlockSpec((SUBLANES, LANES), lambda i: (0, 0))],
        out_shape=[jax.ShapeDtypeStruct((S, D), F32), jax.ShapeDtypeStruct((S, D), F32),
                   jax.ShapeDtypeStruct((S, LANES), I32), jax.ShapeDtypeStruct((S, LANES), F32),
                   jax.ShapeDtypeStruct((SUBLANES, LANES), F32)],
        compiler_params=_cparams(("arbitrary",)),
        name="mix",
    )(attn, delta, x, g_attn, w_out, g_cross, w_q, k_mem, v_mem, w_o, g_moe, w_router, b_router)


def _rank_kernel(eid_ref, cnt_ref, dest_ref, meta_ref, tab_ref, carry, pstart):
    i = pl.program_id(0)
    tm = eid_ref.shape[0]
    lane = lax.broadcasted_iota(I32, (tm, LANES), 1)
    e = eid_ref[...]
    e1 = e[:, 0:1]
    e2 = e[:, 1:2]
    onehot = jnp.where((lane == e1) | (lane == e2), 1.0, 0.0).astype(F32)

    @pl.when(i == 0)
    def _():
        cnt = cnt_ref[...]
        lane8 = lax.broadcasted_iota(I32, cnt.shape, 1)
        is_expert = lane8 < N_EXPERTS

        def prefix_sum(x):
            shift = 1
            while shift < LANES:
                x = x + jnp.where(lane8 >= shift, pltpu.roll(x, shift, 1), 0.0)
                shift *= 2
            return x

        padded = jnp.floor((cnt + (MOE_BLOCK - 1)) * (1.0 / MOE_BLOCK)) * MOE_BLOCK
        pend = prefix_sum(padded)
        pstart[...] = pend - padded
        carry[...] = jnp.zeros_like(carry)

        nb = meta_ref.shape[0]
        blane = lax.broadcasted_iota(I32, (nb, LANES), 1)
        brow = (lax.broadcasted_iota(I32, (nb, LANES), 0) * MOE_BLOCK).astype(F32)
        owner = jnp.sum(jnp.where((blane < N_EXPERTS) & (brow >= pend[0:1, :]), 1.0, 0.0), axis=-1, keepdims=True)
        owner = jnp.minimum(owner, N_EXPERTS - 1.0)
        seg_end = jnp.sum(jnp.where(blane.astype(F32) == owner, (pend - padded + cnt)[0:1, :], 0.0),
                          axis=-1, keepdims=True)
        valid = jnp.clip(seg_end - brow, 0.0, MOE_BLOCK * 1.0)
        meta_ref[...] = jnp.where(blane == 0, owner, jnp.where(blane == 1, valid, 0.0)).astype(I32)

        present = is_expert & (cnt > 0.0)
        order = prefix_sum(jnp.where(present, 1.0, 0.0)) - 1.0
        parity = order - 2.0 * jnp.floor(order * 0.5)
        nxt = jnp.where(present, lane8.astype(F32), N_EXPERTS * 1.0)
        nxt = jnp.where(lane8 < LANES - 1, pltpu.roll(nxt, LANES - 1, 1), N_EXPERTS * 1.0)
        shift = 1
        while shift < LANES:
            nxt = jnp.minimum(nxt, jnp.where(lane8 < LANES - shift, pltpu.roll(nxt, LANES - shift, 1), N_EXPERTS * 1.0))
            shift *= 2
        nxt = jnp.where(nxt >= N_EXPERTS, -1.0, nxt)
        sub8 = lax.broadcasted_iota(I32, cnt.shape, 0)
        first_block = (pend - padded) * (1.0 / MOE_BLOCK)
        pad_lo = pend - padded + cnt
        pad_hi = jnp.where(is_expert, pend, nb * MOE_BLOCK * 1.0)
        rows = (parity, nxt, first_block, pad_lo, pad_hi)
        tab = jnp.zeros_like(cnt)
        for k, v in enumerate(rows):
            tab = jnp.where(sub8 == k, v, tab)
        tab_ref[...] = tab.astype(I32)

    r = lax.broadcasted_iota(I32, (tm, tm), 0)
    c = lax.broadcasted_iota(I32, (tm, tm), 1)
    before = jnp.where(r > c, 1.0, 0.0).astype(BF16)
    pos = jnp.dot(before, onehot.astype(BF16), preferred_element_type=F32) + carry[0:1, :] + pstart[0:1, :]
    d1 = jnp.sum(jnp.where(lane == e1, pos, 0.0), axis=-1, keepdims=True)
    d2 = jnp.sum(jnp.where(lane == e2, pos, 0.0), axis=-1, keepdims=True)
    dest = jnp.where(lane == 0, d1, jnp.where(lane == 1, d2, 0.0))
    dest_ref[...] = dest.T[:SUBLANES].astype(I32)
    carry[...] = carry[...] + jnp.sum(onehot, axis=0, keepdims=True)


def _rank(eid, cnt, n_blocks, tm=256):
    S = eid.shape[0]
    const = lambda shape: pl.BlockSpec(shape, lambda i: (0, 0))
    return pl.pallas_call(
        _rank_kernel,
        grid=(S // tm,),
        in_specs=[pl.BlockSpec((tm, LANES), lambda i: (i, 0)), const((SUBLANES, LANES))],
        out_specs=[pl.BlockSpec((SUBLANES, tm), lambda i: (0, i)), const((n_blocks, LANES)), const((SUBLANES, LANES))],
        out_shape=[jax.ShapeDtypeStruct((SUBLANES, S), I32), jax.ShapeDtypeStruct((n_blocks, LANES), I32),
                   jax.ShapeDtypeStruct((SUBLANES, LANES), I32)],
        scratch_shapes=[pltpu.VMEM((SUBLANES, LANES), F32), pltpu.VMEM((SUBLANES, LANES), F32)],
        compiler_params=_cparams(("arbitrary",)),
        name="rank",
    )(eid, cnt)


def _source_kernel(dest_ref, pad_lo_ref, pad_hi_ref, src_ref, *, n_tok):
    def clear(p, carry):
        src_ref[p] = 0
        return carry

    def clear_segment(e, carry):
        lax.fori_loop(pad_lo_ref[e], pad_hi_ref[e], clear, 0)
        return carry

    lax.fori_loop(0, N_EXPERTS + 1, clear_segment, 0)

    def place(t, carry):
        for k in range(TOP_K):
            src_ref[dest_ref[k * n_tok + t]] = t
        return carry

    lax.fori_loop(0, n_tok, place, 0, unroll=ROW_DMA_UNROLL)


def _source_rows(dest_flat, pad_lo, pad_hi, n_tok, n_rows):
    return pl.pallas_call(
        functools.partial(_source_kernel, n_tok=n_tok),
        grid_spec=pltpu.PrefetchScalarGridSpec(
            num_scalar_prefetch=3,
            grid=(1,),
            in_specs=[],
            out_specs=pl.BlockSpec(memory_space=pltpu.SMEM)),
        out_shape=jax.ShapeDtypeStruct((n_rows,), I32),
        compiler_params=_cparams(("arbitrary",)),
        name="source",
    )(dest_flat, pad_lo, pad_hi)


def _expert_kernel(bexp_ref, nval_ref, slot_ref, next_ref, first_ref, src_ref, hn_ref, wg_ref, wu_ref, wd_ref, y_ref,
                   xbuf, wg_f, wu_f, wd_f, wg_s, wu_s, wd_s, sem, xsem):
    b = pl.program_id(0)
    e = bexp_ref[b]
    e_prev = bexp_ref[jnp.maximum(b - 1, 0)]
    nval = nval_ref[b]

    def gather_block(blk, xslot):
        def issue(r, carry):
            tok = src_ref[blk * MOE_BLOCK + r]
            pltpu.make_async_copy(hn_ref.at[pl.ds(tok, 1), :], xbuf.at[xslot, pl.ds(r, 1), :], xsem.at[xslot]).start()
            return carry

        lax.fori_loop(0, MOE_BLOCK, issue, 0, unroll=ROW_DMA_UNROLL)

    def gather_wait(xslot):
        pltpu.make_async_copy(hn_ref.at[pl.ds(0, MOE_BLOCK), :], xbuf.at[xslot], xsem.at[xslot]).wait()

    @pl.when(jnp.logical_and(b == 0, nval > 0))
    def _():
        gather_block(0, 0)

    @pl.when(jnp.logical_and(nval == 0, nval_ref[jnp.maximum(b - 1, 0)] > 0))
    def _():
        gather_wait(b % 2)

    def weight_copy(w, ex, slot):
        src, dst = ((wg_ref, wg_f), (wu_ref, wu_f), (wd_ref, wd_f))[w]
        return pltpu.make_async_copy(src.at[ex], dst.at[slot], sem.at[slot, w])

    @pl.when(jnp.logical_and(b == 0, nval > 0))
    def _():
        for w in range(3):
            weight_copy(w, e, slot_ref[e]).start()

    @pl.when(jnp.logical_and(nval > 0, jnp.logical_or(b == 0, e != e_prev)))
    def _():
        slot = slot_ref[e]
        for w in range(3):
            weight_copy(w, e, slot).wait()
        wg_s[...] = wg_f[slot].astype(BF16)
        wu_s[...] = wu_f[slot].astype(BF16)
        wd_s[...] = wd_f[slot].astype(BF16)

    @pl.when(nval == 0)
    def _():
        y_ref[...] = jnp.zeros_like(y_ref)

    @pl.when(nval > 0)
    def _():
        xslot = b % 2
        gather_wait(xslot)
        for r in range(MOE_BLOCK):
            tok = src_ref[(b + 1) * MOE_BLOCK + r]
            pltpu.make_async_copy(hn_ref.at[pl.ds(tok, 1), :], xbuf.at[1 - xslot, pl.ds(r, 1), :],
                                  xsem.at[1 - xslot]).start()
        nxt = next_ref[e]
        j = b - first_ref[e]
        last = jnp.logical_or(bexp_ref[b + 1] != e, nval_ref[b + 1] == 0)
        for w in range(3):
            @pl.when(jnp.logical_and(nxt >= 0, jnp.logical_or(j == w, jnp.logical_and(last, j < w))))
            def _(w=w):
                weight_copy(w, nxt, 1 - slot_ref[e]).start()

        x = xbuf[xslot].astype(BF16)
        gate = jnp.dot(x, wg_s[...], preferred_element_type=F32)
        up = jnp.dot(x, wu_s[...], preferred_element_type=F32)
        hid = gate * jax.nn.sigmoid(gate) * up
        y_ref[...] = jnp.dot(hid.astype(BF16), wd_s[...], preferred_element_type=F32)


def _experts(block_expert, block_valid, expert_slot, expert_next, expert_first, src_rows, hn, w_gate, w_up, w_down):
    D = hn.shape[1]
    P = src_rows.shape[0]
    DE = w_gate.shape[-1]
    nb = P // MOE_BLOCK
    hbm = pl.BlockSpec(memory_space=pl.ANY)
    return pl.pallas_call(
        _expert_kernel,
        grid_spec=pltpu.PrefetchScalarGridSpec(
            num_scalar_prefetch=6,
            grid=(nb,),
            in_specs=[hbm, hbm, hbm, hbm],
            out_specs=pl.BlockSpec((MOE_BLOCK, D), lambda b, *_: (b, 0)),
            scratch_shapes=[pltpu.VMEM((2, MOE_BLOCK, D), F32),
                            pltpu.VMEM((2, D, DE), F32), pltpu.VMEM((2, D, DE), F32), pltpu.VMEM((2, DE, D), F32),
                            pltpu.VMEM((D, DE), BF16), pltpu.VMEM((D, DE), BF16), pltpu.VMEM((DE, D), BF16),
                            pltpu.SemaphoreType.DMA((2, 3)), pltpu.SemaphoreType.DMA((2,))]),
        out_shape=jax.ShapeDtypeStruct((P, D), F32),
        compiler_params=_cparams(("arbitrary",)),
        name="experts",
    )(block_expert, block_valid, expert_slot, expert_next, expert_first, src_rows, hn, w_gate, w_up, w_down)


def _combine_kernel(dest_ref, h2_ref, wts_ref, g_ref, yb_ref, o_ref, buf, sem):
    i = pl.program_id(0)
    n = pl.num_programs(0)
    tm = h2_ref.shape[0]
    n_tok = n * tm

    def gather_block(blk, slot):
        def issue(r, carry):
            for k in range(TOP_K):
                d = dest_ref[k * n_tok + blk * tm + r]
                pltpu.make_async_copy(yb_ref.at[pl.ds(d, 1), :], buf.at[slot, k, pl.ds(r, 1), :], sem.at[slot]).start()
            return carry

        lax.fori_loop(0, tm, issue, 0, unroll=ROW_DMA_UNROLL)

    @pl.when(i == 0)
    def _():
        gather_block(0, 0)

    @pl.when(i + 1 < n)
    def _():
        gather_block(i + 1, (i + 1) % 2)

    slot = i % 2
    for k in range(TOP_K):
        pltpu.make_async_copy(yb_ref.at[pl.ds(0, tm), :], buf.at[slot, k], sem.at[slot]).wait()
    wts = wts_ref[...]
    h = h2_ref[...] + wts[:, 0:1] * buf[slot, 0] + wts[:, 1:2] * buf[slot, 1]
    o_ref[...] = _rms(h, g_ref[...])


def _combine(dest_flat, h2, wts, g_final, yb, tm=256):
    S, D = h2.shape
    return pl.pallas_call(
        _combine_kernel,
        grid_spec=pltpu.PrefetchScalarGridSpec(
            num_scalar_prefetch=1,
            grid=(S // tm,),
            in_specs=[pl.BlockSpec((tm, D), lambda i, d: (i, 0)),
                      pl.BlockSpec((tm, LANES), lambda i, d: (i, 0)),
                      pl.BlockSpec((1, D), lambda i, d: (0, 0)),
                      pl.BlockSpec(memory_space=pl.ANY)],
            out_specs=pl.BlockSpec((tm, D), lambda i, d: (i, 0)),
            scratch_shapes=[pltpu.VMEM((2, TOP_K, tm, D), F32), pltpu.SemaphoreType.DMA((2,))]),
        out_shape=jax.ShapeDtypeStruct((S, D), F32),
        compiler_params=_cparams(("arbitrary",)),
        name="combine",
    )(dest_flat, h2, wts, g_final, yb)


def _lane_pad(v, offset):
    return jnp.zeros((1, LANES), F32).at[0, offset:offset + v.shape[0]].set(v.astype(F32))


def _layer(h, mem, g_mix, w_in, conv_w, a_log, dt_bias, g_delta_out, g_attn_out, w_out, g_cross, g_mem,
           w_q_mem, w_kv_mem, w_o_mem, g_moe, w_group, b_group, w_expert, b_expert, w_gate, w_up, w_down,
           g_final):
    S, D = h.shape
    H = N_HEADS_B
    w_main, w_small = _cast_w_in(w_in.T)
    proj, ba = _inproj(h, g_mix[None], w_main, w_small)

    slopes = jnp.asarray(2.0 ** (-8.0 * np.arange(1, N_HEADS_A + 1) / N_HEADS_A), dtype=F32)
    attn = _dilated_attention(proj, slopes)
    delta = _gated_deltanet(proj, ba, conv_w, _lane_pad(a_log, H), _lane_pad(dt_bias, H), g_delta_out[None])

    k_mem, v_mem = _memkv(mem, g_mem[None], w_kv_mem.astype(BF16))
    w_router = _split_hi_lo(jnp.pad(jnp.concatenate([w_group, w_expert], axis=1),
                                    ((0, 0), (0, LANES - N_GROUPS - N_EXPERTS))))
    b_router = _lane_pad(jnp.concatenate([b_group, b_expert]), 0)
    h2, hn, eid, wts, cnt = _mix(attn, delta, h, g_attn_out[None], w_out.astype(BF16), g_cross[None],
                                 w_q_mem.astype(BF16), k_mem, v_mem, w_o_mem.astype(BF16), g_moe[None],
                                 w_router, b_router)

    n_rows = S * TOP_K + N_EXPERTS * MOE_BLOCK
    dest, meta, tab = _rank(eid, cnt, n_rows // MOE_BLOCK)
    dest_flat = dest[:TOP_K].reshape(TOP_K * S)
    src_rows = _source_rows(dest_flat, tab[3, :N_EXPERTS + 1], tab[4, :N_EXPERTS + 1], S, n_rows)
    yb = _experts(meta[:, 0], meta[:, 1], tab[0, :N_EXPERTS], tab[1, :N_EXPERTS], tab[2, :N_EXPERTS], src_rows, hn,
                  w_gate, w_up, w_down)
    return _combine(dest_flat, h2, wts, g_final[None], yb)


def kernel(x, mem, g_mix, w_in, conv_w, a_log, dt_bias, g_delta_out, g_attn_out, w_out, g_cross, g_mem, w_q_mem,
           w_kv_mem, w_o_mem, g_moe, w_group, b_group, w_expert, b_expert, w_gate, w_up, w_down, g_final):
    assert x.shape[0] == 1 and mem.shape[0] == 1 and g_mix.shape[0] == 1
    out = _layer(x[0].astype(F32), mem[0].astype(F32), g_mix[0], w_in[0], conv_w[0], a_log[0], dt_bias[0],
                 g_delta_out[0], g_attn_out[0], w_out[0], g_cross[0], g_mem[0], w_q_mem[0], w_kv_mem[0],
                 w_o_mem[0], g_moe[0], w_group[0], b_group[0], w_expert[0], b_expert[0], w_gate[0], w_up[0],
                 w_down[0], g_final)
    return out[None].astype(x.dtype)
```

```python
import numpy as np
import jax
import jax.numpy as jnp
from jax import lax
from jax.experimental import pallas as pl
from jax.experimental.pallas import tpu as pltpu

F32 = jnp.float32
BF16 = jnp.bfloat16
I32 = jnp.int32
HIGHEST = lax.Precision.HIGHEST

EPS = 1e-6
HEAD_DIM = 128
LANES = 128
SUBLANES = 8
NEG = -1e30

N_HEADS_A = 8
N_HEADS_B = 8
W_A = N_HEADS_A * HEAD_DIM
W_B = N_HEADS_B * HEAD_DIM
W_MAIN = 3 * W_A + 4 * W_B
DILATED_PATTERNS = ((128, 1), (512, 4), (2048, 16))
ATTN_BLOCK = 128
ATTN_TILE = 2048
CONV_WIDTH = 4
DELTA_CHUNK = 64
DELTA_BLOCK = 256
N_MEM_HEADS = 4
W_MEM = N_MEM_HEADS * HEAD_DIM
N_GROUPS = 8
EXPERTS_PER_GROUP = 8
N_EXPERTS = 64
TOP_K = 2
MOE_BLOCK = 128
ROW_DMA_UNROLL = 16
GATHER_AHEAD = 2
SPARE_BLOCKS = 8

VMEM_LIMIT = 56 * 1024 * 1024


def _cparams(sem):
    return pltpu.CompilerParams(dimension_semantics=sem, vmem_limit_bytes=VMEM_LIMIT)


def _rms(x, g):
    return x * lax.rsqrt(jnp.mean(x * x, axis=-1, keepdims=True) + EPS) * g


def _dot(a, b):
    return jnp.dot(a.astype(BF16), b.astype(BF16), preferred_element_type=F32)


def _dot_nt(a, b):
    return lax.dot_general(a.astype(BF16), b.astype(BF16), (((1,), (1,)), ((), ())), preferred_element_type=F32)


def _dot_tn(a, b):
    return lax.dot_general(a.astype(BF16), b.astype(BF16), (((0,), (0,)), ((), ())), preferred_element_type=F32)


def _split_hi_lo(w):
    hi = w.astype(BF16)
    lo = (w - hi.astype(F32)).astype(BF16)
    return jnp.concatenate([hi, lo], axis=1)


def _dot_x3(x, w_hl_ref):
    x_hi = x.astype(BF16)
    x_lo = (x - x_hi.astype(F32)).astype(BF16)
    r = jnp.dot(x_hi, w_hl_ref[...], preferred_element_type=F32)
    return r[:, :LANES] + r[:, LANES:] + jnp.dot(x_lo, w_hl_ref[:, :LANES], preferred_element_type=F32)


def _castw_kernel(w_ref, ws_ref, o_ref, os_ref):
    o_ref[...] = w_ref[...].astype(BF16)

    @pl.when(pl.program_id(0) == 0)
    def _():
        w = ws_ref[...]
        hi = w.astype(BF16)
        os_ref[...] = jnp.zeros_like(os_ref)
        os_ref[0:w.shape[0], :] = hi
        os_ref[LANES:LANES + w.shape[0], :] = (w - hi.astype(F32)).astype(BF16)


def _cast_w_in(w_t, tn=1024):
    D = w_t.shape[1]
    n_gate = 2 * N_HEADS_B
    assert w_t.shape[0] == W_MAIN + n_gate and W_MAIN % tn == 0 and W_MAIN % n_gate == 0
    return pl.pallas_call(
        _castw_kernel,
        grid=(W_MAIN // tn,),
        in_specs=[pl.BlockSpec((tn, D), lambda j: (j, 0)),
                  pl.BlockSpec((n_gate, D), lambda j: (W_MAIN // n_gate, 0))],
        out_specs=[pl.BlockSpec((tn, D), lambda j: (j, 0)),
                   pl.BlockSpec((2 * LANES, D), lambda j: (0, 0))],
        out_shape=[jax.ShapeDtypeStruct((W_MAIN, D), BF16), jax.ShapeDtypeStruct((2 * LANES, D), BF16)],
        compiler_params=_cparams(("arbitrary",)),
        name="castw",
    )(w_t, w_t)


def _inproj_kernel(x_ref, g_ref, w_ref, ws_ref, o_ref, os_ref, u_scr):
    @pl.when(pl.program_id(1) == 0)
    def _():
        u = _rms(x_ref[...], g_ref[...])
        u_hi = u.astype(BF16)
        u_scr[...] = u_hi
        u_lo = (u - u_hi.astype(F32)).astype(BF16)
        r = _dot_nt(u_hi, ws_ref[...])
        os_ref[...] = r[:, :LANES] + r[:, LANES:] + _dot_nt(u_lo, ws_ref[0:LANES, :])

    o_ref[...] = _dot_nt(u_scr[...], w_ref[...])


def _inproj(x, g, w_main, w_small, tm=1024, tn=1024):
    S, D = x.shape
    N = w_main.shape[0]
    return pl.pallas_call(
        _inproj_kernel,
        grid=(S // tm, N // tn),
        in_specs=[pl.BlockSpec((tm, D), lambda i, j: (i, 0)),
                  pl.BlockSpec((1, D), lambda i, j: (0, 0)),
                  pl.BlockSpec((tn, D), lambda i, j: (j, 0)),
                  pl.BlockSpec((2 * LANES, D), lambda i, j: (0, 0))],
        out_specs=[pl.BlockSpec((tm, tn), lambda i, j: (i, j)),
                   pl.BlockSpec((tm, LANES), lambda i, j: (i, 0))],
        out_shape=[jax.ShapeDtypeStruct((S, N), F32), jax.ShapeDtypeStruct((S, LANES), F32)],
        scratch_shapes=[pltpu.VMEM((tm, D), BF16)],
        compiler_params=_cparams(("parallel", "arbitrary")),
        name="inproj",
    )(x, g, w_main, w_small)


def _attn_kernel(slope_ref, q_ref, kp_ref, kc_ref, vp_ref, vc_ref, o_ref, kk, vv, o_scr, l_scr):
    i = pl.program_id(0)
    h = pl.program_id(1)
    T = ATTN_TILE
    B = ATTN_BLOCK
    slope = slope_ref[h]
    kk[0:T, :] = kp_ref[...]
    kk[T:2 * T, :] = kc_ref[...]
    vv[0:T, :] = vp_ref[...]
    vv[T:2 * T, :] = vc_ref[...]
    qi = lax.broadcasted_iota(I32, (B, 2 * B), 0)
    ki = lax.broadcasted_iota(I32, (B, 2 * B), 1)
    dist = qi + B - ki
    band = (dist >= 0) & (dist <= B)
    band_first = band & (ki >= B)
    distf = dist.astype(F32)
    scale = HEAD_DIM ** -0.5

    for p, (window, d) in enumerate(DILATED_PATTERNS):
        assert window // d == B and T % (B * d) == 0
        bias = jnp.where(band, distf * (-slope * d), NEG)
        bias_first = jnp.where(band_first, distf * (-slope * d), NEG)

        def body(b, carry, p=p, d=d, bias=bias, bias_first=bias_first):
            r = b % d
            j = b // d
            qs = r + B * d * j
            ks = T - B * d + qs
            q = q_ref[pl.ds(qs, B, stride=d), :] * scale
            k = kk[pl.ds(ks, 2 * B, stride=d), :]
            v = vv[pl.ds(ks, 2 * B, stride=d), :]
            s = _dot_nt(q, k)
            first = jnp.logical_and(i == 0, j == 0)
            s = s + jnp.where(first, bias_first, bias)
            m = jnp.max(s, axis=-1, keepdims=True)
            e = jnp.exp(s - m)
            l = jnp.sum(e, axis=-1, keepdims=True)
            o = _dot(e, v) / l
            lse = m + jnp.log(l)
            o_scr[p, pl.ds(qs, B, stride=d), :] = o
            l_scr[p, pl.ds(qs, B, stride=d), :] = jnp.broadcast_to(lse, (B, HEAD_DIM))
            return carry

        lax.fori_loop(0, T // B, body, 0, unroll=8)

    l0, l1, l2 = l_scr[0], l_scr[1], l_scr[2]
    m = jnp.maximum(jnp.maximum(l0, l1), l2)
    w0, w1, w2 = jnp.exp(l0 - m), jnp.exp(l1 - m), jnp.exp(l2 - m)
    o_ref[...] = (w0 * o_scr[0] + w1 * o_scr[1] + w2 * o_scr[2]) / (w0 + w1 + w2)


def _dilated_attention(proj, slopes):
    S = proj.shape[0]
    T = ATTN_TILE
    H = N_HEADS_A
    blk = (T, HEAD_DIM)
    return pl.pallas_call(
        _attn_kernel,
        grid_spec=pltpu.PrefetchScalarGridSpec(
            num_scalar_prefetch=1,
            grid=(S // T, H),
            in_specs=[pl.BlockSpec(blk, lambda i, h, s: (i, h)),
                      pl.BlockSpec(blk, lambda i, h, s: (jnp.maximum(i - 1, 0), H + h)),
                      pl.BlockSpec(blk, lambda i, h, s: (i, H + h)),
                      pl.BlockSpec(blk, lambda i, h, s: (jnp.maximum(i - 1, 0), 2 * H + h)),
                      pl.BlockSpec(blk, lambda i, h, s: (i, 2 * H + h))],
            out_specs=pl.BlockSpec(blk, lambda i, h, s: (i, h)),
            scratch_shapes=[pltpu.VMEM((2 * T, HEAD_DIM), F32), pltpu.VMEM((2 * T, HEAD_DIM), F32),
                            pltpu.VMEM((3, T, HEAD_DIM), F32), pltpu.VMEM((3, T, HEAD_DIM), F32)]),
        out_shape=jax.ShapeDtypeStruct((S, W_A), F32),
        compiler_params=_cparams(("parallel", "parallel")),
        name="attn",
    )(slopes, proj, proj, proj, proj, proj)


def _softplus(x):
    return jnp.maximum(x, 0.0) + jnp.log1p(jnp.exp(-jnp.abs(x)))


def _delta_kernel(hist_ref, qkv_ref, z_ref, ba_ref, cw_ref, alog_ref, dtb_ref, gout_ref, o_ref,
                  ext, act, gct, state):
    i = pl.program_id(0)
    CB = DELTA_BLOCK
    C = DELTA_CHUNK
    H = N_HEADS_B
    D = HEAD_DIM

    @pl.when(i == 0)
    def _():
        state[...] = jnp.zeros_like(state)
        ext[0:SUBLANES, :] = jnp.zeros((SUBLANES, 3 * W_B), F32)

    @pl.when(i > 0)
    def _():
        ext[0:SUBLANES, :] = hist_ref[...]

    ext[SUBLANES:SUBLANES + CB, :] = qkv_ref[...]

    for c in range(3 * H):
        sl = slice(c * D, (c + 1) * D)
        acc = cw_ref[CONV_WIDTH - 1:CONV_WIDTH, sl] * ext[SUBLANES:SUBLANES + CB, sl]
        for t in range(1, CONV_WIDTH):
            acc = acc + cw_ref[CONV_WIDTH - 1 - t:CONV_WIDTH - t, sl] * ext[SUBLANES - t:SUBLANES - t + CB, sl]
        a = acc * jax.nn.sigmoid(acc)
        if c < 2 * H:
            a = a * lax.rsqrt(jnp.sum(a * a, axis=-1, keepdims=True) + EPS)
        if c < H:
            a = a * (D ** -0.5)
        act[:, sl] = a

    ba = ba_ref[...]
    beta = jax.nn.sigmoid(ba)
    g = -jnp.exp(alog_ref[...]) * _softplus(ba + dtb_ref[...])
    row = lax.broadcasted_iota(I32, (CB, CB), 0)
    col = lax.broadcasted_iota(I32, (CB, CB), 1)
    chunk_tri = jnp.where((row // C == col // C) & (row >= col), 1.0, 0.0).astype(F32)
    gc = jnp.dot(chunk_tri, g, precision=HIGHEST, preferred_element_type=F32)
    gct[...] = gc.T
    eg = jnp.exp(gc)

    ri = lax.broadcasted_iota(I32, (C, C), 0)
    ci = lax.broadcasted_iota(I32, (C, C), 1)
    causal = ri >= ci
    strict = ri > ci
    gout = gout_ref[...]

    heads = range(H)
    for n in range(CB // C):
        rows = slice(n * C, (n + 1) * C)
        q = [act[rows, h * D:(h + 1) * D] for h in heads]
        k = [act[rows, (H + h) * D:(H + h + 1) * D] for h in heads]
        v = [act[rows, (2 * H + h) * D:(2 * H + h + 1) * D] for h in heads]
        beta_c = [beta[rows, h:h + 1] for h in heads]
        gc_c = [gc[rows, H + h:H + h + 1] for h in heads]
        eg_c = [eg[rows, H + h:H + h + 1] for h in heads]
        gc_r = [gct[H + h:H + h + 1, rows] for h in heads]
        g_last = [gc_r[h][:, C - 1:C] for h in heads]
        decay = [jnp.exp(jnp.where(causal, gc_c[h] - gc_r[h], NEG)) for h in heads]
        kb = [k[h] * beta_c[h] for h in heads]
        kq = [_dot_nt(jnp.concatenate([kb[h], q[h]], axis=0), k[h]) for h in heads]
        qk = [jnp.where(causal, kq[h][C:] * decay[h], 0.0) for h in heads]
        pw = [jnp.where(strict, -kq[h][:C] * decay[h], 0.0) for h in heads]
        r = pw
        pw = [_dot(pw[h], pw[h]) for h in heads]
        for _ in range(int(np.log2(C)) - 2):
            pr = [_dot(jnp.concatenate([pw[h], r[h]], axis=0), pw[h]) for h in heads]
            r = [r[h] + pw[h] + pr[h][C:] for h in heads]
            pw = [pr[h][:C] for h in heads]
        r = [r[h] + pw[h] + _dot(r[h], pw[h]) for h in heads]
        bmat = [jnp.concatenate([v[h] * beta_c[h], kb[h] * eg_c[h]], axis=-1) for h in heads]
        uw = [bmat[h] + _dot(r[h], bmat[h]) for h in heads]
        wq = [jnp.concatenate([uw[h][:, D:], q[h] * eg_c[h]], axis=0) for h in heads]
        kd = [k[h] * jnp.exp(g_last[h] - gc_c[h]) for h in heads]
        s_old = [state[h] for h in heads]
        ws = [_dot(wq[h], s_old[h]) for h in heads]
        v_new = [uw[h][:, :D] - ws[h][:C] for h in heads]
        for h in heads:
            state[h] = s_old[h] * jnp.exp(g_last[h]) + _dot_tn(kd[h], v_new[h])
        o = [ws[h][C:] + _dot(qk[h], v_new[h]) for h in heads]
        for h in heads:
            z = z_ref[rows, h * D:(h + 1) * D]
            o_ref[rows, h * D:(h + 1) * D] = _rms(o[h], gout) * (z * jax.nn.sigmoid(z))


def _gated_deltanet(proj, ba, conv_w, alog_pad, dtb_pad, g_out):
    S = proj.shape[0]
    CB = DELTA_BLOCK
    qkv_col = W_A * 3 // (3 * W_B)
    z_col = (3 * W_A + 3 * W_B) // W_B
    return pl.pallas_call(
        _delta_kernel,
        grid=(S // CB,),
        in_specs=[pl.BlockSpec((SUBLANES, 3 * W_B), lambda i: (jnp.maximum(i * (CB // SUBLANES) - 1, 0), qkv_col)),
                  pl.BlockSpec((CB, 3 * W_B), lambda i: (i, qkv_col)),
                  pl.BlockSpec((CB, W_B), lambda i: (i, z_col)),
                  pl.BlockSpec((CB, LANES), lambda i: (i, 0)),
                  pl.BlockSpec((CONV_WIDTH, 3 * W_B), lambda i: (0, 0)),
                  pl.BlockSpec((1, LANES), lambda i: (0, 0)),
                  pl.BlockSpec((1, LANES), lambda i: (0, 0)),
                  pl.BlockSpec((1, HEAD_DIM), lambda i: (0, 0))],
        out_specs=pl.BlockSpec((CB, W_B), lambda i: (i, 0)),
        out_shape=jax.ShapeDtypeStruct((S, W_B), F32),
        scratch_shapes=[pltpu.VMEM((CB + SUBLANES, 3 * W_B), F32),
                        pltpu.VMEM((CB, 3 * W_B), F32),
                        pltpu.VMEM((LANES, CB), F32),
                        pltpu.VMEM((N_HEADS_B, HEAD_DIM, HEAD_DIM), F32)],
        compiler_params=_cparams(("arbitrary",)),
        name="delta",
    )(proj, proj, proj, ba, conv_w, alog_pad, dtb_pad, g_out)


def _memkv_kernel(mem_ref, g_ref, w_ref, k_ref, v_ref):
    kv = _dot(_rms(mem_ref[...], g_ref[...]), w_ref[...])
    k_ref[...] = kv[:, :W_MEM].astype(BF16)
    v_ref[...] = kv[:, W_MEM:].astype(BF16)


def _memkv(mem, g, w_kv):
    M = mem.shape[0]
    return pl.pallas_call(
        _memkv_kernel,
        out_shape=[jax.ShapeDtypeStruct((M, W_MEM), BF16), jax.ShapeDtypeStruct((M, W_MEM), BF16)],
        compiler_params=pltpu.CompilerParams(vmem_limit_bytes=VMEM_LIMIT),
        name="memkv",
    )(mem, g, w_kv)


def _mix_kernel(attn_ref, delta_ref, x_ref, gattn_ref, wout_ref, gcross_ref, wq_ref, km_ref, vm_ref, wo_ref,
                gmoe_ref, wr_ref, br_ref, h2_ref, hn_ref, eid_ref, wts_ref, cnt_ref):
    an = _rms(attn_ref[...], gattn_ref[...])
    mix = jnp.concatenate([an.astype(BF16), delta_ref[...].astype(BF16)], axis=-1)
    h1 = x_ref[...] + jnp.dot(mix, wout_ref[...], preferred_element_type=F32)

    q = _dot(_rms(h1, gcross_ref[...]), wq_ref[...]) * (HEAD_DIM ** -0.5)
    outs = []
    for hh in range(N_MEM_HEADS):
        sl = slice(hh * HEAD_DIM, (hh + 1) * HEAD_DIM)
        s = _dot_nt(q[:, sl], km_ref[:, sl])
        e = jnp.exp(s - jnp.max(s, axis=-1, keepdims=True))
        outs.append(_dot(e, vm_ref[:, sl]) / jnp.sum(e, axis=-1, keepdims=True))
    h2 = h1 + _dot(jnp.concatenate(outs, axis=-1), wo_ref[...])
    h2_ref[...] = h2

    hn = _rms(h2, gmoe_ref[...])
    hn_ref[...] = hn
    logits = _dot_x3(hn, wr_ref) + br_ref[...]
    lane = lax.broadcasted_iota(I32, logits.shape, 1)
    gl = jnp.where(lane < N_GROUPS, logits, NEG)
    gmax = jnp.max(gl, axis=-1, keepdims=True)
    g_sel = jnp.min(jnp.where(gl == gmax, lane, LANES), axis=-1, keepdims=True)
    g_gate = 1.0 / jnp.sum(jnp.exp(gl - gmax), axis=-1, keepdims=True)
    in_group = (lane >= N_GROUPS) & ((lane - N_GROUPS) // EXPERTS_PER_GROUP == g_sel)
    el = jnp.where(in_group, logits, NEG)
    v1 = jnp.max(el, axis=-1, keepdims=True)
    i1 = jnp.min(jnp.where(in_group & (el == v1), lane, LANES), axis=-1, keepdims=True)
    in_rest = in_group & (lane != i1)
    el2 = jnp.where(in_rest, logits, NEG)
    v2 = jnp.max(el2, axis=-1, keepdims=True)
    i2 = jnp.min(jnp.where(in_rest & (el2 == v2), lane, LANES), axis=-1, keepdims=True)
    e2 = jnp.exp(v2 - v1)
    w1 = g_gate / (1.0 + e2)
    w2 = g_gate * e2 / (1.0 + e2)
    eid_ref[...] = jnp.where(lane == 0, i1 - N_GROUPS, jnp.where(lane == 1, i2 - N_GROUPS, 0))
    wts_ref[...] = jnp.where(lane == 0, w1, jnp.where(lane == 1, w2, 0.0))

    @pl.when(pl.program_id(0) == 0)
    def _():
        cnt_ref[...] = jnp.zeros_like(cnt_ref)

    picked = (lane == i1 - N_GROUPS) | (lane == i2 - N_GROUPS)
    cnt_ref[...] = cnt_ref[...] + jnp.sum(jnp.where(picked, 1.0, 0.0), axis=0, keepdims=True)


def _mix(attn, delta, x, g_attn, w_out, g_cross, w_q, k_mem, v_mem, w_o, g_moe, w_router, b_router, tm=512):
    S, D = x.shape
    row = lambda w: pl.BlockSpec((tm, w), lambda i: (i, 0))
    full = lambda a: pl.BlockSpec(a.shape, lambda i: (0, 0), pipeline_mode=pl.Buffered(1))
    return pl.pallas_call(
        _mix_kernel,
        grid=(S // tm,),
        in_specs=[row(W_A), row(W_B), row(D), full(g_attn), full(w_out), full(g_cross), full(w_q),
                  full(k_mem), full(v_mem), full(w_o), full(g_moe), full(w_router), full(b_router)],
        out_specs=[row(D), row(D), row(LANES), row(LANES), pl.BlockSpec((SUBLANES, LANES), lambda i: (0, 0))],
        out_shape=[jax.ShapeDtypeStruct((S, D), F32), jax.ShapeDtypeStruct((S, D), F32),
                   jax.ShapeDtypeStruct((S, LANES), I32), jax.ShapeDtypeStruct((S, LANES), F32),
                   jax.ShapeDtypeStruct((SUBLANES, LANES), F32)],
        compiler_params=_cparams(("arbitrary",)),
        name="mix",
    )(attn, delta, x, g_attn, w_out, g_cross, w_q, k_mem, v_mem, w_o, g_moe, w_router, b_router)


def _rank_kernel(eid_ref, cnt_ref, dest_ref, meta_ref, tab_ref, carry, pstart):
    i = pl.program_id(0)
    tm = eid_ref.shape[0]
    lane = lax.broadcasted_iota(I32, (tm, LANES), 1)
    e = eid_ref[...]
    e1 = e[:, 0:1]
    e2 = e[:, 1:2]
    onehot = jnp.where((lane == e1) | (lane == e2), 1.0, 0.0).astype(F32)

    @pl.when(i == 0)
    def _():
        cnt = cnt_ref[...]
        lane8 = lax.broadcasted_iota(I32, cnt.shape, 1)
        is_expert = lane8 < N_EXPERTS

        def prefix_sum(x):
            shift = 1
            while shift < LANES:
                x = x + jnp.where(lane8 >= shift, pltpu.roll(x, shift, 1), 0.0)
                shift *= 2
            return x

        padded = jnp.floor((cnt + (MOE_BLOCK - 1)) * (1.0 / MOE_BLOCK)) * MOE_BLOCK
        pend = prefix_sum(padded)
        pstart[...] = pend - padded
        carry[...] = jnp.zeros_like(carry)

        nb = meta_ref.shape[0]
        blane = lax.broadcasted_iota(I32, (nb, LANES), 1)
        brow = (lax.broadcasted_iota(I32, (nb, LANES), 0) * MOE_BLOCK).astype(F32)
        owner = jnp.sum(jnp.where((blane < N_EXPERTS) & (brow >= pend[0:1, :]), 1.0, 0.0), axis=-1, keepdims=True)
        owner = jnp.minimum(owner, N_EXPERTS - 1.0)
        seg_end = jnp.sum(jnp.where(blane.astype(F32) == owner, (pend - padded + cnt)[0:1, :], 0.0),
                          axis=-1, keepdims=True)
        valid = jnp.clip(seg_end - brow, 0.0, MOE_BLOCK * 1.0)
        meta_ref[...] = jnp.where(blane == 0, owner, jnp.where(blane == 1, valid, 0.0)).astype(I32)

        present = is_expert & (cnt > 0.0)
        order = prefix_sum(jnp.where(present, 1.0, 0.0)) - 1.0
        parity = order - 2.0 * jnp.floor(order * 0.5)
        nxt = jnp.where(present, lane8.astype(F32), N_EXPERTS * 1.0)
        nxt = jnp.where(lane8 < LANES - 1, pltpu.roll(nxt, LANES - 1, 1), N_EXPERTS * 1.0)
        shift = 1
        while shift < LANES:
            nxt = jnp.minimum(nxt, jnp.where(lane8 < LANES - shift, pltpu.roll(nxt, LANES - shift, 1), N_EXPERTS * 1.0))
            shift *= 2
        nxt = jnp.where(nxt >= N_EXPERTS, -1.0, nxt)
        sub8 = lax.broadcasted_iota(I32, cnt.shape, 0)
        first_block = (pend - padded) * (1.0 / MOE_BLOCK)
        pad_lo = pend - padded + cnt
        pad_hi = jnp.where(is_expert, pend, nb * MOE_BLOCK * 1.0)
        rows = (parity, nxt, first_block, pad_lo, pad_hi)
        tab = jnp.zeros_like(cnt)
        for k, v in enumerate(rows):
            tab = jnp.where(sub8 == k, v, tab)
        tab_ref[...] = tab.astype(I32)

    r = lax.broadcasted_iota(I32, (tm, tm), 0)
    c = lax.broadcasted_iota(I32, (tm, tm), 1)
    before = jnp.where(r > c, 1.0, 0.0).astype(BF16)
    pos = jnp.dot(before, onehot.astype(BF16), preferred_element_type=F32) + carry[0:1, :] + pstart[0:1, :]
    d1 = jnp.sum(jnp.where(lane == e1, pos, 0.0), axis=-1, keepdims=True)
    d2 = jnp.sum(jnp.where(lane == e2, pos, 0.0), axis=-1, keepdims=True)
    dest = jnp.where(lane == 0, d1, jnp.where(lane == 1, d2, 0.0))
    dest_ref[...] = dest.T[:SUBLANES].astype(I32)
    carry[...] = carry[...] + jnp.sum(onehot, axis=0, keepdims=True)


def _rank(eid, cnt, n_blocks, tm=256):
    S = eid.shape[0]
    const = lambda shape: pl.BlockSpec(shape, lambda i: (0, 0))
    return pl.pallas_call(
        _rank_kernel,
        grid=(S // tm,),
        in_specs=[pl.BlockSpec((tm, LANES), lambda i: (i, 0)), const((SUBLANES, LANES))],
        out_specs=[pl.BlockSpec((SUBLANES, tm), lambda i: (0, i)), const((n_blocks, LANES)), const((SUBLANES, LANES))],
        out_shape=[jax.ShapeDtypeStruct((SUBLANES, S), I32), jax.ShapeDtypeStruct((n_blocks, LANES), I32),
                   jax.ShapeDtypeStruct((SUBLANES, LANES), I32)],
        scratch_shapes=[pltpu.VMEM((SUBLANES, LANES), F32), pltpu.VMEM((SUBLANES, LANES), F32)],
        compiler_params=_cparams(("arbitrary",)),
        name="rank",
    )(eid, cnt)


def _fill_source(dest_ref, pad_lo_ref, pad_hi_ref, src_ref):
    n_tok = dest_ref.shape[0] // TOP_K

    def clear(p, carry):
        src_ref[p] = 0
        return carry

    def clear_segment(e, carry):
        lax.fori_loop(pad_lo_ref[e], pad_hi_ref[e], clear, 0)
        return carry

    lax.fori_loop(0, N_EXPERTS + 1, clear_segment, 0)

    def place(t, carry):
        for k in range(TOP_K):
            src_ref[dest_ref[k * n_tok + t]] = t
        return carry

    lax.fori_loop(0, n_tok, place, 0, unroll=ROW_DMA_UNROLL)


def _expert_kernel(bexp_ref, nval_ref, slot_ref, next_ref, first_ref, dest_ref, pad_lo_ref, pad_hi_ref,
                   hn_ref, wg_ref, wu_ref, wd_ref, y_ref,
                   src_ref, xbuf, wg_f, wu_f, wd_f, wg_s, wu_s, wd_s, sem, xsem):
    b = pl.program_id(0)
    e = bexp_ref[b]
    e_prev = bexp_ref[jnp.maximum(b - 1, 0)]
    nval = nval_ref[b]

    def weight_copy(w, ex, slot):
        src, dst = ((wg_ref, wg_f), (wu_ref, wu_f), (wd_ref, wd_f))[w]
        return pltpu.make_async_copy(src.at[ex], dst.at[slot], sem.at[slot, w])

    @pl.when(b == 0)
    def _():
        @pl.when(nval > 0)
        def _():
            for w in range(3):
                weight_copy(w, e, slot_ref[e]).start()

        _fill_source(dest_ref, pad_lo_ref, pad_hi_ref, src_ref)

    def gather_block(blk, xslot):
        def issue(r, carry):
            tok = src_ref[blk * MOE_BLOCK + r]
            pltpu.make_async_copy(hn_ref.at[pl.ds(tok, 1), :], xbuf.at[xslot, pl.ds(r, 1), :], xsem.at[xslot]).start()
            return carry

        lax.fori_loop(0, MOE_BLOCK, issue, 0, unroll=ROW_DMA_UNROLL)

    def gather_wait(xslot):
        pltpu.make_async_copy(hn_ref.at[pl.ds(0, MOE_BLOCK), :], xbuf.at[xslot], xsem.at[xslot]).wait()

    n_slots = GATHER_AHEAD + 1

    @pl.when(jnp.logical_and(b == 0, nval > 0))
    def _():
        for k in range(GATHER_AHEAD):
            gather_block(k, k)

    gathered = jnp.where(b < GATHER_AHEAD, nval_ref[0], nval_ref[jnp.maximum(b - GATHER_AHEAD, 0)]) > 0

    @pl.when(jnp.logical_and(nval == 0, gathered))
    def _():
        gather_wait(b % n_slots)

    @pl.when(jnp.logical_and(nval > 0, jnp.logical_or(b == 0, e != e_prev)))
    def _():
        slot = slot_ref[e]
        for w in range(3):
            weight_copy(w, e, slot).wait()
        wg_s[...] = wg_f[slot].astype(BF16)
        wu_s[...] = wu_f[slot].astype(BF16)
        wd_s[...] = wd_f[slot].astype(BF16)

    @pl.when(nval == 0)
    def _():
        y_ref[...] = jnp.zeros_like(y_ref)

    @pl.when(nval > 0)
    def _():
        xslot = b % n_slots
        gather_wait(xslot)
        nslot = (b + GATHER_AHEAD) % n_slots
        for r in range(MOE_BLOCK):
            tok = src_ref[(b + GATHER_AHEAD) * MOE_BLOCK + r]
            pltpu.make_async_copy(hn_ref.at[pl.ds(tok, 1), :], xbuf.at[nslot, pl.ds(r, 1), :],
                                  xsem.at[nslot]).start()
        nxt = next_ref[e]
        j = b - first_ref[e]
        last = jnp.logical_or(bexp_ref[b + 1] != e, nval_ref[b + 1] == 0)
        for w in range(3):
            @pl.when(jnp.logical_and(nxt >= 0, jnp.logical_or(j == w, jnp.logical_and(last, j < w))))
            def _(w=w):
                weight_copy(w, nxt, 1 - slot_ref[e]).start()

        x = xbuf[xslot].astype(BF16)
        gate = jnp.dot(x, wg_s[...], preferred_element_type=F32)
        up = jnp.dot(x, wu_s[...], preferred_element_type=F32)
        hid = gate * jax.nn.sigmoid(gate) * up
        y_ref[...] = jnp.dot(hid.astype(BF16), wd_s[...], preferred_element_type=F32)


def _experts(block_expert, block_valid, expert_slot, expert_next, expert_first, dest_flat, pad_lo, pad_hi,
             hn, w_gate, w_up, w_down):
    D = hn.shape[1]
    nb = block_expert.shape[0]
    P = nb * MOE_BLOCK
    DE = w_gate.shape[-1]
    hbm = pl.BlockSpec(memory_space=pl.ANY)
    return pl.pallas_call(
        _expert_kernel,
        grid_spec=pltpu.PrefetchScalarGridSpec(
            num_scalar_prefetch=8,
            grid=(nb,),
            in_specs=[hbm, hbm, hbm, hbm],
            out_specs=pl.BlockSpec((MOE_BLOCK, D), lambda b, *_: (b, 0)),
            scratch_shapes=[pltpu.SMEM((P,), I32),
                            pltpu.VMEM((GATHER_AHEAD + 1, MOE_BLOCK, D), F32),
                            pltpu.VMEM((2, D, DE), F32), pltpu.VMEM((2, D, DE), F32), pltpu.VMEM((2, DE, D), F32),
                            pltpu.VMEM((D, DE), BF16), pltpu.VMEM((D, DE), BF16), pltpu.VMEM((DE, D), BF16),
                            pltpu.SemaphoreType.DMA((2, 3)), pltpu.SemaphoreType.DMA((GATHER_AHEAD + 1,))]),
        out_shape=jax.ShapeDtypeStruct((P, D), F32),
        compiler_params=_cparams(("arbitrary",)),
        name="experts",
    )(block_expert, block_valid, expert_slot, expert_next, expert_first, dest_flat, pad_lo, pad_hi,
      hn, w_gate, w_up, w_down)


def _combine_kernel(dest_ref, h2_ref, wts_ref, g_ref, yb_ref, o_ref, buf, sem):
    i = pl.program_id(0)
    n = pl.num_programs(0)
    tm = h2_ref.shape[0]
    n_tok = n * tm

    def gather_block(blk, slot):
        def issue(r, carry):
            for k in range(TOP_K):
                d = dest_ref[k * n_tok + blk * tm + r]
                pltpu.make_async_copy(yb_ref.at[pl.ds(d, 1), :], buf.at[slot, k, pl.ds(r, 1), :], sem.at[slot]).start()
            return carry

        lax.fori_loop(0, tm, issue, 0, unroll=ROW_DMA_UNROLL)

    @pl.when(i == 0)
    def _():
        gather_block(0, 0)

    @pl.when(i + 1 < n)
    def _():
        gather_block(i + 1, (i + 1) % 2)

    slot = i % 2
    for k in range(TOP_K):
        pltpu.make_async_copy(yb_ref.at[pl.ds(0, tm), :], buf.at[slot, k], sem.at[slot]).wait()
    wts = wts_ref[...]
    h = h2_ref[...] + wts[:, 0:1] * buf[slot, 0] + wts[:, 1:2] * buf[slot, 1]
    o_ref[...] = _rms(h, g_ref[...])


def _combine(dest_flat, h2, wts, g_final, yb, tm=256):
    S, D = h2.shape
    return pl.pallas_call(
        _combine_kernel,
        grid_spec=pltpu.PrefetchScalarGridSpec(
            num_scalar_prefetch=1,
            grid=(S // tm,),
            in_specs=[pl.BlockSpec((tm, D), lambda i, d: (i, 0)),
                      pl.BlockSpec((tm, LANES), lambda i, d: (i, 0)),
                      pl.BlockSpec((1, D), lambda i, d: (0, 0)),
                      pl.BlockSpec(memory_space=pl.ANY)],
            out_specs=pl.BlockSpec((tm, D), lambda i, d: (i, 0)),
            scratch_shapes=[pltpu.VMEM((2, TOP_K, tm, D), F32), pltpu.SemaphoreType.DMA((2,))]),
        out_shape=jax.ShapeDtypeStruct((S, D), F32),
        compiler_params=_cparams(("arbitrary",)),
        name="combine",
    )(dest_flat, h2, wts, g_final, yb)


def _lane_pad(v, offset):
    return jnp.zeros((1, LANES), F32).at[0, offset:offset + v.shape[0]].set(v.astype(F32))


def _layer(h, mem, g_mix, w_in, conv_w, a_log, dt_bias, g_delta_out, g_attn_out, w_out, g_cross, g_mem,
           w_q_mem, w_kv_mem, w_o_mem, g_moe, w_group, b_group, w_expert, b_expert, w_gate, w_up, w_down,
           g_final):
    S, D = h.shape
    H = N_HEADS_B
    w_main, w_small = _cast_w_in(w_in.T)
    proj, ba = _inproj(h, g_mix[None], w_main, w_small)

    slopes = jnp.asarray(2.0 ** (-8.0 * np.arange(1, N_HEADS_A + 1) / N_HEADS_A), dtype=F32)
    attn = _dilated_attention(proj, slopes)
    delta = _gated_deltanet(proj, ba, conv_w, _lane_pad(a_log, H), _lane_pad(dt_bias, H), g_delta_out[None])

    k_mem, v_mem = _memkv(mem, g_mem[None], w_kv_mem.astype(BF16))
    w_router = _split_hi_lo(jnp.pad(jnp.concatenate([w_group, w_expert], axis=1),
                                    ((0, 0), (0, LANES - N_GROUPS - N_EXPERTS))))
    b_router = _lane_pad(jnp.concatenate([b_group, b_expert]), 0)
    h2, hn, eid, wts, cnt = _mix(attn, delta, h, g_attn_out[None], w_out.astype(BF16), g_cross[None],
                                 w_q_mem.astype(BF16), k_mem, v_mem, w_o_mem.astype(BF16), g_moe[None],
                                 w_router, b_router)

    n_rows = S * TOP_K + (N_EXPERTS + SPARE_BLOCKS) * MOE_BLOCK
    dest, meta, tab = _rank(eid, cnt, n_rows // MOE_BLOCK)
    dest_flat = dest[:TOP_K].reshape(TOP_K * S)
    yb = _experts(meta[:, 0], meta[:, 1], tab[0, :N_EXPERTS], tab[1, :N_EXPERTS], tab[2, :N_EXPERTS], dest_flat,
                  tab[3, :N_EXPERTS + 1], tab[4, :N_EXPERTS + 1], hn, w_gate, w_up, w_down)
    return _combine(dest_flat, h2, wts, g_final[None], yb)


def kernel(x, mem, g_mix, w_in, conv_w, a_log, dt_bias, g_delta_out, g_attn_out, w_out, g_cross, g_mem, w_q_mem,
           w_kv_mem, w_o_mem, g_moe, w_group, b_group, w_expert, b_expert, w_gate, w_up, w_down, g_final):
    assert x.shape[0] == 1 and mem.shape[0] == 1 and g_mix.shape[0] == 1
    out = _layer(x[0].astype(F32), mem[0].astype(F32), g_mix[0], w_in[0], conv_w[0], a_log[0], dt_bias[0],
                 g_delta_out[0], g_attn_out[0], w_out[0], g_cross[0], g_mem[0], w_q_mem[0], w_kv_mem[0],
                 w_o_mem[0], g_moe[0], w_group[0], b_group[0], w_expert[0], b_expert[0], w_gate[0], w_up[0],
                 w_down[0], g_final)
    return out[None].astype(x.dtype)
```

```python
import numpy as np
import jax
import jax.numpy as jnp
from jax import lax
from jax.experimental import pallas as pl
from jax.experimental.pallas import tpu as pltpu

F32 = jnp.float32
BF16 = jnp.bfloat16
I32 = jnp.int32
HIGHEST = lax.Precision.HIGHEST

EPS = 1e-6
HEAD_DIM = 128
LANES = 128
SUBLANES = 8
NEG = -1e30

N_HEADS_A = 8
N_HEADS_B = 8
W_A = N_HEADS_A * HEAD_DIM
W_B = N_HEADS_B * HEAD_DIM
W_MAIN = 3 * W_A + 4 * W_B
DILATED_PATTERNS = ((128, 1), (512, 4), (2048, 16))
ATTN_BLOCK = 128
ATTN_TILE = 2048
CONV_WIDTH = 4
DELTA_CHUNK = 64
DELTA_BLOCK = 256
N_MEM_HEADS = 4
W_MEM = N_MEM_HEADS * HEAD_DIM
N_GROUPS = 8
EXPERTS_PER_GROUP = 8
N_EXPERTS = 64
TOP_K = 2
MOE_BLOCK = 128
ROW_DMA_UNROLL = 16

VMEM_LIMIT = 56 * 1024 * 1024


def _cparams(sem):
    return pltpu.CompilerParams(dimension_semantics=sem, vmem_limit_bytes=VMEM_LIMIT)


def _rms(x, g):
    return x * lax.rsqrt(jnp.mean(x * x, axis=-1, keepdims=True) + EPS) * g


def _dot(a, b):
    return jnp.dot(a.astype(BF16), b.astype(BF16), preferred_element_type=F32)


def _dot_nt(a, b):
    return lax.dot_general(a.astype(BF16), b.astype(BF16), (((1,), (1,)), ((), ())), preferred_element_type=F32)


def _dot_tn(a, b):
    return lax.dot_general(a.astype(BF16), b.astype(BF16), (((0,), (0,)), ((), ())), preferred_element_type=F32)


def _split_hi_lo(w):
    hi = w.astype(BF16)
    lo = (w - hi.astype(F32)).astype(BF16)
    return jnp.concatenate([hi, lo], axis=1)


def _dot_x3(x, w_hl_ref):
    x_hi = x.astype(BF16)
    x_lo = (x - x_hi.astype(F32)).astype(BF16)
    r = jnp.dot(x_hi, w_hl_ref[...], preferred_element_type=F32)
    return r[:, :LANES] + r[:, LANES:] + jnp.dot(x_lo, w_hl_ref[:, :LANES], preferred_element_type=F32)


def _castw_kernel(w_ref, ws_ref, o_ref, os_ref):
    o_ref[...] = w_ref[...].astype(BF16)

    @pl.when(pl.program_id(0) == 0)
    def _():
        w = ws_ref[...]
        hi = w.astype(BF16)
        os_ref[...] = jnp.zeros_like(os_ref)
        os_ref[0:w.shape[0], :] = hi
        os_ref[LANES:LANES + w.shape[0], :] = (w - hi.astype(F32)).astype(BF16)


def _cast_w_in(w_t, tn=1024):
    D = w_t.shape[1]
    n_gate = 2 * N_HEADS_B
    assert w_t.shape[0] == W_MAIN + n_gate and W_MAIN % tn == 0 and W_MAIN % n_gate == 0
    return pl.pallas_call(
        _castw_kernel,
        grid=(W_MAIN // tn,),
        in_specs=[pl.BlockSpec((tn, D), lambda j: (j, 0)),
                  pl.BlockSpec((n_gate, D), lambda j: (W_MAIN // n_gate, 0))],
        out_specs=[pl.BlockSpec((tn, D), lambda j: (j, 0)),
                   pl.BlockSpec((2 * LANES, D), lambda j: (0, 0))],
        out_shape=[jax.ShapeDtypeStruct((W_MAIN, D), BF16), jax.ShapeDtypeStruct((2 * LANES, D), BF16)],
        compiler_params=_cparams(("arbitrary",)),
        name="castw",
    )(w_t, w_t)


def _inproj_kernel(x_ref, g_ref, w_ref, ws_ref, o_ref, os_ref, u_scr):
    @pl.when(pl.program_id(1) == 0)
    def _():
        u = _rms(x_ref[...], g_ref[...])
        u_hi = u.astype(BF16)
        u_scr[...] = u_hi
        u_lo = (u - u_hi.astype(F32)).astype(BF16)
        r = _dot_nt(u_hi, ws_ref[...])
        os_ref[...] = r[:, :LANES] + r[:, LANES:] + _dot_nt(u_lo, ws_ref[0:LANES, :])

    o_ref[...] = _dot_nt(u_scr[...], w_ref[...])


def _inproj(x, g, w_main, w_small, tm=1024, tn=1024):
    S, D = x.shape
    N = w_main.shape[0]
    return pl.pallas_call(
        _inproj_kernel,
        grid=(S // tm, N // tn),
        in_specs=[pl.BlockSpec((tm, D), lambda i, j: (i, 0)),
                  pl.BlockSpec((1, D), lambda i, j: (0, 0)),
                  pl.BlockSpec((tn, D), lambda i, j: (j, 0)),
                  pl.BlockSpec((2 * LANES, D), lambda i, j: (0, 0))],
        out_specs=[pl.BlockSpec((tm, tn), lambda i, j: (i, j)),
                   pl.BlockSpec((tm, LANES), lambda i, j: (i, 0))],
        out_shape=[jax.ShapeDtypeStruct((S, N), F32), jax.ShapeDtypeStruct((S, LANES), F32)],
        scratch_shapes=[pltpu.VMEM((tm, D), BF16)],
        compiler_params=_cparams(("parallel", "arbitrary")),
        name="inproj",
    )(x, g, w_main, w_small)


def _attn_kernel(slope_ref, q_ref, kp_ref, kc_ref, vp_ref, vc_ref, o_ref, kk, vv, o_scr, l_scr):
    i = pl.program_id(0)
    h = pl.program_id(1)
    T = ATTN_TILE
    B = ATTN_BLOCK
    slope = slope_ref[h]
    kk[0:T, :] = kp_ref[...]
    kk[T:2 * T, :] = kc_ref[...]
    vv[0:T, :] = vp_ref[...]
    vv[T:2 * T, :] = vc_ref[...]
    qi = lax.broadcasted_iota(I32, (B, 2 * B), 0)
    ki = lax.broadcasted_iota(I32, (B, 2 * B), 1)
    dist = qi + B - ki
    band = (dist >= 0) & (dist <= B)
    band_first = band & (ki >= B)
    distf = dist.astype(F32)
    scale = HEAD_DIM ** -0.5

    for p, (window, d) in enumerate(DILATED_PATTERNS):
        assert window // d == B and T % (B * d) == 0
        bias = jnp.where(band, distf * (-slope * d), NEG)
        bias_first = jnp.where(band_first, distf * (-slope * d), NEG)

        def body(b, carry, p=p, d=d, bias=bias, bias_first=bias_first):
            r = b % d
            j = b // d
            qs = r + B * d * j
            ks = T - B * d + qs
            q = q_ref[pl.ds(qs, B, stride=d), :] * scale
            k = kk[pl.ds(ks, 2 * B, stride=d), :]
            v = vv[pl.ds(ks, 2 * B, stride=d), :]
            s = _dot_nt(q, k)
            first = jnp.logical_and(i == 0, j == 0)
            s = s + jnp.where(first, bias_first, bias)
            m = jnp.max(s, axis=-1, keepdims=True)
            e = jnp.exp(s - m)
            l = jnp.sum(e, axis=-1, keepdims=True)
            o = _dot(e, v) / l
            lse = m + jnp.log(l)
            o_scr[p, pl.ds(qs, B, stride=d), :] = o
            l_scr[p, pl.ds(qs, B, stride=d), :] = jnp.broadcast_to(lse, (B, HEAD_DIM))
            return carry

        lax.fori_loop(0, T // B, body, 0, unroll=8)

    l0, l1, l2 = l_scr[0], l_scr[1], l_scr[2]
    m = jnp.maximum(jnp.maximum(l0, l1), l2)
    w0, w1, w2 = jnp.exp(l0 - m), jnp.exp(l1 - m), jnp.exp(l2 - m)
    o_ref[...] = (w0 * o_scr[0] + w1 * o_scr[1] + w2 * o_scr[2]) / (w0 + w1 + w2)


def _dilated_attention(proj, slopes):
    S = proj.shape[0]
    T = ATTN_TILE
    H = N_HEADS_A
    blk = (T, HEAD_DIM)
    return pl.pallas_call(
        _attn_kernel,
        grid_spec=pltpu.PrefetchScalarGridSpec(
            num_scalar_prefetch=1,
            grid=(S // T, H),
            in_specs=[pl.BlockSpec(blk, lambda i, h, s: (i, h)),
                      pl.BlockSpec(blk, lambda i, h, s: (jnp.maximum(i - 1, 0), H + h)),
                      pl.BlockSpec(blk, lambda i, h, s: (i, H + h)),
                      pl.BlockSpec(blk, lambda i, h, s: (jnp.maximum(i - 1, 0), 2 * H + h)),
                      pl.BlockSpec(blk, lambda i, h, s: (i, 2 * H + h))],
            out_specs=pl.BlockSpec(blk, lambda i, h, s: (i, h)),
            scratch_shapes=[pltpu.VMEM((2 * T, HEAD_DIM), F32), pltpu.VMEM((2 * T, HEAD_DIM), F32),
                            pltpu.VMEM((3, T, HEAD_DIM), F32), pltpu.VMEM((3, T, HEAD_DIM), F32)]),
        out_shape=jax.ShapeDtypeStruct((S, W_A), F32),
        compiler_params=_cparams(("parallel", "parallel")),
        name="attn",
    )(slopes, proj, proj, proj, proj, proj)


def _softplus(x):
    return jnp.maximum(x, 0.0) + jnp.log1p(jnp.exp(-jnp.abs(x)))


def _delta_kernel(hist_ref, qkv_ref, z_ref, ba_ref, cw_ref, alog_ref, dtb_ref, gout_ref, o_ref,
                  ext, act, gct, state):
    i = pl.program_id(0)
    CB = DELTA_BLOCK
    C = DELTA_CHUNK
    H = N_HEADS_B
    D = HEAD_DIM

    @pl.when(i == 0)
    def _():
        state[...] = jnp.zeros_like(state)
        ext[0:SUBLANES, :] = jnp.zeros((SUBLANES, 3 * W_B), F32)

    @pl.when(i > 0)
    def _():
        ext[0:SUBLANES, :] = hist_ref[...]

    ext[SUBLANES:SUBLANES + CB, :] = qkv_ref[...]

    for c in range(3 * H):
        sl = slice(c * D, (c + 1) * D)
        acc = cw_ref[CONV_WIDTH - 1:CONV_WIDTH, sl] * ext[SUBLANES:SUBLANES + CB, sl]
        for t in range(1, CONV_WIDTH):
            acc = acc + cw_ref[CONV_WIDTH - 1 - t:CONV_WIDTH - t, sl] * ext[SUBLANES - t:SUBLANES - t + CB, sl]
        a = acc * jax.nn.sigmoid(acc)
        if c < 2 * H:
            a = a * lax.rsqrt(jnp.sum(a * a, axis=-1, keepdims=True) + EPS)
        if c < H:
            a = a * (D ** -0.5)
        act[:, sl] = a

    ba = ba_ref[...]
    beta = jax.nn.sigmoid(ba)
    g = -jnp.exp(alog_ref[...]) * _softplus(ba + dtb_ref[...])
    row = lax.broadcasted_iota(I32, (CB, CB), 0)
    col = lax.broadcasted_iota(I32, (CB, CB), 1)
    chunk_tri = jnp.where((row // C == col // C) & (row >= col), 1.0, 0.0).astype(F32)
    gc = jnp.dot(chunk_tri, g, precision=HIGHEST, preferred_element_type=F32)
    gct[...] = gc.T
    eg = jnp.exp(gc)

    ri = lax.broadcasted_iota(I32, (C, C), 0)
    ci = lax.broadcasted_iota(I32, (C, C), 1)
    causal = ri >= ci
    strict = ri > ci
    gout = gout_ref[...]

    heads = range(H)
    for n in range(CB // C):
        rows = slice(n * C, (n + 1) * C)
        q = [act[rows, h * D:(h + 1) * D] for h in heads]
        k = [act[rows, (H + h) * D:(H + h + 1) * D] for h in heads]
        v = [act[rows, (2 * H + h) * D:(2 * H + h + 1) * D] for h in heads]
        beta_c = [beta[rows, h:h + 1] for h in heads]
        gc_c = [gc[rows, H + h:H + h + 1] for h in heads]
        eg_c = [eg[rows, H + h:H + h + 1] for h in heads]
        gc_r = [gct[H + h:H + h + 1, rows] for h in heads]
        g_last = [gc_r[h][:, C - 1:C] for h in heads]
        decay = [jnp.exp(jnp.where(causal, gc_c[h] - gc_r[h], NEG)) for h in heads]
        kb = [k[h] * beta_c[h] for h in heads]
        kq = [_dot_nt(jnp.concatenate([kb[h], q[h]], axis=0), k[h]) for h in heads]
        qk = [jnp.where(causal, kq[h][C:] * decay[h], 0.0) for h in heads]
        pw = [jnp.where(strict, -kq[h][:C] * decay[h], 0.0) for h in heads]
        r = pw
        pw = [_dot(pw[h], pw[h]) for h in heads]
        for _ in range(int(np.log2(C)) - 2):
            pr = [_dot(jnp.concatenate([pw[h], r[h]], axis=0), pw[h]) for h in heads]
            r = [r[h] + pw[h] + pr[h][C:] for h in heads]
            pw = [pr[h][:C] for h in heads]
        r = [r[h] + pw[h] + _dot(r[h], pw[h]) for h in heads]
        bmat = [jnp.concatenate([v[h] * beta_c[h], kb[h] * eg_c[h]], axis=-1) for h in heads]
        uw = [bmat[h] + _dot(r[h], bmat[h]) for h in heads]
        wq = [jnp.concatenate([uw[h][:, D:], q[h] * eg_c[h]], axis=0) for h in heads]
        kd = [k[h] * jnp.exp(g_last[h] - gc_c[h]) for h in heads]
        s_old = [state[h] for h in heads]
        ws = [_dot(wq[h], s_old[h]) for h in heads]
        v_new = [uw[h][:, :D] - ws[h][:C] for h in heads]
        for h in heads:
            state[h] = s_old[h] * jnp.exp(g_last[h]) + _dot_tn(kd[h], v_new[h])
        o = [ws[h][C:] + _dot(qk[h], v_new[h]) for h in heads]
        for h in heads:
            z = z_ref[rows, h * D:(h + 1) * D]
            o_ref[rows, h * D:(h + 1) * D] = _rms(o[h], gout) * (z * jax.nn.sigmoid(z))


def _gated_deltanet(proj, ba, conv_w, alog_pad, dtb_pad, g_out):
    S = proj.shape[0]
    CB = DELTA_BLOCK
    qkv_col = W_A * 3 // (3 * W_B)
    z_col = (3 * W_A + 3 * W_B) // W_B
    return pl.pallas_call(
        _delta_kernel,
        grid=(S // CB,),
        in_specs=[pl.BlockSpec((SUBLANES, 3 * W_B), lambda i: (jnp.maximum(i * (CB // SUBLANES) - 1, 0), qkv_col)),
                  pl.BlockSpec((CB, 3 * W_B), lambda i: (i, qkv_col)),
                  pl.BlockSpec((CB, W_B), lambda i: (i, z_col)),
                  pl.BlockSpec((CB, LANES), lambda i: (i, 0)),
                  pl.BlockSpec((CONV_WIDTH, 3 * W_B), lambda i: (0, 0)),
                  pl.BlockSpec((1, LANES), lambda i: (0, 0)),
                  pl.BlockSpec((1, LANES), lambda i: (0, 0)),
                  pl.BlockSpec((1, HEAD_DIM), lambda i: (0, 0))],
        out_specs=pl.BlockSpec((CB, W_B), lambda i: (i, 0)),
        out_shape=jax.ShapeDtypeStruct((S, W_B), F32),
        scratch_shapes=[pltpu.VMEM((CB + SUBLANES, 3 * W_B), F32),
                        pltpu.VMEM((CB, 3 * W_B), F32),
                        pltpu.VMEM((LANES, CB), F32),
                        pltpu.VMEM((N_HEADS_B, HEAD_DIM, HEAD_DIM), F32)],
        compiler_params=_cparams(("arbitrary",)),
        name="delta",
    )(proj, proj, proj, ba, conv_w, alog_pad, dtb_pad, g_out)


def _memkv_kernel(mem_ref, g_ref, w_ref, k_ref, v_ref):
    kv = _dot(_rms(mem_ref[...], g_ref[...]), w_ref[...])
    k_ref[...] = kv[:, :W_MEM].astype(BF16)
    v_ref[...] = kv[:, W_MEM:].astype(BF16)


def _memkv(mem, g, w_kv):
    M = mem.shape[0]
    return pl.pallas_call(
        _memkv_kernel,
        out_shape=[jax.ShapeDtypeStruct((M, W_MEM), BF16), jax.ShapeDtypeStruct((M, W_MEM), BF16)],
        compiler_params=pltpu.CompilerParams(vmem_limit_bytes=VMEM_LIMIT),
        name="memkv",
    )(mem, g, w_kv)


def _mix_kernel(attn_ref, delta_ref, x_ref, gattn_ref, wout_ref, gcross_ref, wq_ref, km_ref, vm_ref, wo_ref,
                gmoe_ref, wr_ref, br_ref, h2_ref, hn_ref, eid_ref, wts_ref, cnt_ref):
    an = _rms(attn_ref[...], gattn_ref[...])
    mix = jnp.concatenate([an.astype(BF16), delta_ref[...].astype(BF16)], axis=-1)
    h1 = x_ref[...] + jnp.dot(mix, wout_ref[...], preferred_element_type=F32)

    q = _dot(_rms(h1, gcross_ref[...]), wq_ref[...]) * (HEAD_DIM ** -0.5)
    outs = []
    for hh in range(N_MEM_HEADS):
        sl = slice(hh * HEAD_DIM, (hh + 1) * HEAD_DIM)
        s = _dot_nt(q[:, sl], km_ref[:, sl])
        e = jnp.exp(s - jnp.max(s, axis=-1, keepdims=True))
        outs.append(_dot(e, vm_ref[:, sl]) / jnp.sum(e, axis=-1, keepdims=True))
    h2 = h1 + _dot(jnp.concatenate(outs, axis=-1), wo_ref[...])
    h2_ref[...] = h2

    hn = _rms(h2, gmoe_ref[...])
    hn_ref[...] = hn
    logits = _dot_x3(hn, wr_ref) + br_ref[...]
    lane = lax.broadcasted_iota(I32, logits.shape, 1)
    gl = jnp.where(lane < N_GROUPS, logits, NEG)
    gmax = jnp.max(gl, axis=-1, keepdims=True)
    g_sel = jnp.min(jnp.where(gl == gmax, lane, LANES), axis=-1, keepdims=True)
    g_gate = 1.0 / jnp.sum(jnp.exp(gl - gmax), axis=-1, keepdims=True)
    in_group = (lane >= N_GROUPS) & ((lane - N_GROUPS) // EXPERTS_PER_GROUP == g_sel)
    el = jnp.where(in_group, logits, NEG)
    v1 = jnp.max(el, axis=-1, keepdims=True)
    i1 = jnp.min(jnp.where(in_group & (el == v1), lane, LANES), axis=-1, keepdims=True)
    in_rest = in_group & (lane != i1)
    el2 = jnp.where(in_rest, logits, NEG)
    v2 = jnp.max(el2, axis=-1, keepdims=True)
    i2 = jnp.min(jnp.where(in_rest & (el2 == v2), lane, LANES), axis=-1, keepdims=True)
    e2 = jnp.exp(v2 - v1)
    w1 = g_gate / (1.0 + e2)
    w2 = g_gate * e2 / (1.0 + e2)
    eid_ref[...] = jnp.where(lane == 0, i1 - N_GROUPS, jnp.where(lane == 1, i2 - N_GROUPS, 0))
    wts_ref[...] = jnp.where(lane == 0, w1, jnp.where(lane == 1, w2, 0.0))

    @pl.when(pl.program_id(0) == 0)
    def _():
        cnt_ref[...] = jnp.zeros_like(cnt_ref)

    picked = (lane == i1 - N_GROUPS) | (lane == i2 - N_GROUPS)
    cnt_ref[...] = cnt_ref[...] + jnp.sum(jnp.where(picked, 1.0, 0.0), axis=0, keepdims=True)


def _mix(attn, delta, x, g_attn, w_out, g_cross, w_q, k_mem, v_mem, w_o, g_moe, w_router, b_router, tm=512):
    S, D = x.shape
    row = lambda w: pl.BlockSpec((tm, w), lambda i: (i, 0))
    full = lambda a: pl.BlockSpec(a.shape, lambda i: (0, 0), pipeline_mode=pl.Buffered(1))
    return pl.pallas_call(
        _mix_kernel,
        grid=(S // tm,),
        in_specs=[row(W_A), row(W_B), row(D), full(g_attn), full(w_out), full(g_cross), full(w_q),
                  full(k_mem), full(v_mem), full(w_o), full(g_moe), full(w_router), full(b_router)],
        out_specs=[row(D), row(D), row(LANES), row(LANES), pl.BlockSpec((SUBLANES, LANES), lambda i: (0, 0))],
        out_shape=[jax.ShapeDtypeStruct((S, D), F32), jax.ShapeDtypeStruct((S, D), F32),
                   jax.ShapeDtypeStruct((S, LANES), I32), jax.ShapeDtypeStruct((S, LANES), F32),
                   jax.ShapeDtypeStruct((SUBLANES, LANES), F32)],
        compiler_params=_cparams(("arbitrary",)),
        name="mix",
    )(attn, delta, x, g_attn, w_out, g_cross, w_q, k_mem, v_mem, w_o, g_moe, w_router, b_router)


def _rank_kernel(eid_ref, cnt_ref, dest_ref, meta_ref, tab_ref, carry, pstart):
    i = pl.program_id(0)
    tm = eid_ref.shape[0]
    lane = lax.broadcasted_iota(I32, (tm, LANES), 1)
    e = eid_ref[...]
    e1 = e[:, 0:1]
    e2 = e[:, 1:2]
    onehot = jnp.where((lane == e1) | (lane == e2), 1.0, 0.0).astype(F32)

    @pl.when(i == 0)
    def _():
        cnt = cnt_ref[...]
        lane8 = lax.broadcasted_iota(I32, cnt.shape, 1)
        is_expert = lane8 < N_EXPERTS

        def prefix_sum(x):
            shift = 1
            while shift < LANES:
                x = x + jnp.where(lane8 >= shift, pltpu.roll(x, shift, 1), 0.0)
                shift *= 2
            return x

        padded = jnp.floor((cnt + (MOE_BLOCK - 1)) * (1.0 / MOE_BLOCK)) * MOE_BLOCK
        pend = prefix_sum(padded)
        pstart[...] = pend - padded
        carry[...] = jnp.zeros_like(carry)

        nb = meta_ref.shape[0]
        blane = lax.broadcasted_iota(I32, (nb, LANES), 1)
        brow = (lax.broadcasted_iota(I32, (nb, LANES), 0) * MOE_BLOCK).astype(F32)
        owner = jnp.sum(jnp.where((blane < N_EXPERTS) & (brow >= pend[0:1, :]), 1.0, 0.0), axis=-1, keepdims=True)
        owner = jnp.minimum(owner, N_EXPERTS - 1.0)
        seg_end = jnp.sum(jnp.where(blane.astype(F32) == owner, (pend - padded + cnt)[0:1, :], 0.0),
                          axis=-1, keepdims=True)
        valid = jnp.clip(seg_end - brow, 0.0, MOE_BLOCK * 1.0)
        meta_ref[...] = jnp.where(blane == 0, owner, jnp.where(blane == 1, valid, 0.0)).astype(I32)

        present = is_expert & (cnt > 0.0)
        order = prefix_sum(jnp.where(present, 1.0, 0.0)) - 1.0
        parity = order - 2.0 * jnp.floor(order * 0.5)
        nxt = jnp.where(present, lane8.astype(F32), N_EXPERTS * 1.0)
        nxt = jnp.where(lane8 < LANES - 1, pltpu.roll(nxt, LANES - 1, 1), N_EXPERTS * 1.0)
        shift = 1
        while shift < LANES:
            nxt = jnp.minimum(nxt, jnp.where(lane8 < LANES - shift, pltpu.roll(nxt, LANES - shift, 1), N_EXPERTS * 1.0))
            shift *= 2
        nxt = jnp.where(nxt >= N_EXPERTS, -1.0, nxt)
        sub8 = lax.broadcasted_iota(I32, cnt.shape, 0)
        first_block = (pend - padded) * (1.0 / MOE_BLOCK)
        rows = (parity, nxt, first_block)
        tab = jnp.zeros_like(cnt)
        for k, v in enumerate(rows):
            tab = jnp.where(sub8 == k, v, tab)
        tab_ref[...] = tab.astype(I32)

    r = lax.broadcasted_iota(I32, (tm, tm), 0)
    c = lax.broadcasted_iota(I32, (tm, tm), 1)
    before = jnp.where(r > c, 1.0, 0.0).astype(BF16)
    pos = jnp.dot(before, onehot.astype(BF16), preferred_element_type=F32) + carry[0:1, :] + pstart[0:1, :]
    d1 = jnp.sum(jnp.where(lane == e1, pos, 0.0), axis=-1, keepdims=True)
    d2 = jnp.sum(jnp.where(lane == e2, pos, 0.0), axis=-1, keepdims=True)
    dest = jnp.where(lane == 0, d1, jnp.where(lane == 1, d2, 0.0))
    dest_ref[...] = dest.T[:SUBLANES].astype(I32)
    carry[...] = carry[...] + jnp.sum(onehot, axis=0, keepdims=True)


def _rank(eid, cnt, n_blocks, tm=256):
    S = eid.shape[0]
    const = lambda shape: pl.BlockSpec(shape, lambda i: (0, 0))
    return pl.pallas_call(
        _rank_kernel,
        grid=(S // tm,),
        in_specs=[pl.BlockSpec((tm, LANES), lambda i: (i, 0)), const((SUBLANES, LANES))],
        out_specs=[pl.BlockSpec((SUBLANES, tm), lambda i: (0, i)), const((n_blocks, LANES)), const((SUBLANES, LANES))],
        out_shape=[jax.ShapeDtypeStruct((SUBLANES, S), I32), jax.ShapeDtypeStruct((n_blocks, LANES), I32),
                   jax.ShapeDtypeStruct((SUBLANES, LANES), I32)],
        scratch_shapes=[pltpu.VMEM((SUBLANES, LANES), F32), pltpu.VMEM((SUBLANES, LANES), F32)],
        compiler_params=_cparams(("arbitrary",)),
        name="rank",
    )(eid, cnt)


def _scatter_kernel(dest_ref, nval_ref, hn_ref, xb_ref, zeros, sem):
    i = pl.program_id(0)
    tm = hn_ref.shape[0]

    @pl.when(i == 0)
    def _():
        zeros[...] = jnp.zeros_like(zeros)

        def zero_block(b):
            return pltpu.make_async_copy(zeros, xb_ref.at[pl.ds(b * MOE_BLOCK, MOE_BLOCK), :], sem)

        def zissue(b, carry):
            @pl.when(nval_ref[b] < MOE_BLOCK)
            def _():
                zero_block(b).start()
            return carry

        def zdrain(b, carry):
            @pl.when(nval_ref[b] < MOE_BLOCK)
            def _():
                zero_block(b).wait()
            return carry

        lax.fori_loop(0, nval_ref.shape[0], zissue, 0)
        lax.fori_loop(0, nval_ref.shape[0], zdrain, 0)

    n_tok = pl.num_programs(0) * tm

    def issue(r, carry):
        for k in range(TOP_K):
            d = dest_ref[k * n_tok + i * tm + r]
            pltpu.make_async_copy(hn_ref.at[pl.ds(r, 1), :], xb_ref.at[pl.ds(d, 1), :], sem).start()
        return carry

    lax.fori_loop(0, tm, issue, 0, unroll=ROW_DMA_UNROLL)
    for k in range(TOP_K):
        pltpu.make_async_copy(hn_ref, xb_ref.at[pl.ds(0, tm), :], sem).wait()


def _scatter_rows(dest_flat, block_valid, hn, n_rows, tm=512):
    S, D = hn.shape
    return pl.pallas_call(
        _scatter_kernel,
        grid_spec=pltpu.PrefetchScalarGridSpec(
            num_scalar_prefetch=2,
            grid=(S // tm,),
            in_specs=[pl.BlockSpec((tm, D), lambda i, d, nv: (i, 0))],
            out_specs=pl.BlockSpec(memory_space=pl.ANY),
            scratch_shapes=[pltpu.VMEM((MOE_BLOCK, D), F32), pltpu.SemaphoreType.DMA(())]),
        out_shape=jax.ShapeDtypeStruct((n_rows, D), F32),
        compiler_params=_cparams(("arbitrary",)),
        name="scatter",
    )(dest_flat, block_valid, hn)


def _expert_kernel(bexp_ref, nval_ref, slot_ref, next_ref, first_ref, x_ref, wg_ref, wu_ref, wd_ref, y_ref,
                   wg_f, wu_f, wd_f, wg_s, wu_s, wd_s, sem):
    b = pl.program_id(0)
    nb = pl.num_programs(0)
    e = bexp_ref[b]
    e_prev = bexp_ref[jnp.maximum(b - 1, 0)]
    nval = nval_ref[b]

    def weight_copy(w, ex, slot):
        src, dst = ((wg_ref, wg_f), (wu_ref, wu_f), (wd_ref, wd_f))[w]
        return pltpu.make_async_copy(src.at[ex], dst.at[slot], sem.at[slot, w])

    @pl.when(jnp.logical_and(b == 0, nval > 0))
    def _():
        for w in range(3):
            weight_copy(w, e, slot_ref[e]).start()

    @pl.when(jnp.logical_and(nval > 0, jnp.logical_or(b == 0, e != e_prev)))
    def _():
        slot = slot_ref[e]
        for w in range(3):
            weight_copy(w, e, slot).wait()
        wg_s[...] = wg_f[slot].astype(BF16)
        wu_s[...] = wu_f[slot].astype(BF16)
        wd_s[...] = wd_f[slot].astype(BF16)

    @pl.when(nval == 0)
    def _():
        y_ref[...] = jnp.zeros_like(y_ref)

    @pl.when(nval > 0)
    def _():
        nxt = next_ref[e]
        j = b - first_ref[e]
        b1 = jnp.minimum(b + 1, nb - 1)
        last = jnp.logical_or(b + 1 == nb, jnp.logical_or(bexp_ref[b1] != e, nval_ref[b1] == 0))
        for w in range(3):
            @pl.when(jnp.logical_and(nxt >= 0, jnp.logical_or(j == w, jnp.logical_and(last, j < w))))
            def _(w=w):
                weight_copy(w, nxt, 1 - slot_ref[e]).start()

        x = x_ref[...].astype(BF16)
        gate = jnp.dot(x, wg_s[...], preferred_element_type=F32)
        up = jnp.dot(x, wu_s[...], preferred_element_type=F32)
        hid = gate * jax.nn.sigmoid(gate) * up
        y_ref[...] = jnp.dot(hid.astype(BF16), wd_s[...], preferred_element_type=F32)


def _experts(block_expert, block_valid, expert_slot, expert_next, expert_first, xb, w_gate, w_up, w_down):
    P, D = xb.shape
    DE = w_gate.shape[-1]
    row_blk = pl.BlockSpec((MOE_BLOCK, D), lambda b, *_: (b, 0))
    hbm = pl.BlockSpec(memory_space=pl.ANY)
    return pl.pallas_call(
        _expert_kernel,
        grid_spec=pltpu.PrefetchScalarGridSpec(
            num_scalar_prefetch=5,
            grid=(P // MOE_BLOCK,),
            in_specs=[row_blk, hbm, hbm, hbm],
            out_specs=row_blk,
            scratch_shapes=[pltpu.VMEM((2, D, DE), F32), pltpu.VMEM((2, D, DE), F32), pltpu.VMEM((2, DE, D), F32),
                            pltpu.VMEM((D, DE), BF16), pltpu.VMEM((D, DE), BF16), pltpu.VMEM((DE, D), BF16),
                            pltpu.SemaphoreType.DMA((2, 3))]),
        out_shape=jax.ShapeDtypeStruct((P, D), F32),
        compiler_params=_cparams(("arbitrary",)),
        name="experts",
    )(block_expert, block_valid, expert_slot, expert_next, expert_first, xb, w_gate, w_up, w_down)


def _combine_kernel(dest_ref, h2_ref, wts_ref, g_ref, yb_ref, o_ref, buf, sem):
    i = pl.program_id(0)
    n = pl.num_programs(0)
    tm = h2_ref.shape[0]
    n_tok = n * tm

    def gather_block(blk, slot):
        def issue(r, carry):
            for k in range(TOP_K):
                d = dest_ref[k * n_tok + blk * tm + r]
                pltpu.make_async_copy(yb_ref.at[pl.ds(d, 1), :], buf.at[slot, k, pl.ds(r, 1), :], sem.at[slot]).start()
            return carry

        lax.fori_loop(0, tm, issue, 0, unroll=ROW_DMA_UNROLL)

    @pl.when(i == 0)
    def _():
        gather_block(0, 0)

    @pl.when(i + 1 < n)
    def _():
        gather_block(i + 1, (i + 1) % 2)

    slot = i % 2
    for k in range(TOP_K):
        pltpu.make_async_copy(yb_ref.at[pl.ds(0, tm), :], buf.at[slot, k], sem.at[slot]).wait()
    wts = wts_ref[...]
    h = h2_ref[...] + wts[:, 0:1] * buf[slot, 0] + wts[:, 1:2] * buf[slot, 1]
    o_ref[...] = _rms(h, g_ref[...])


def _combine(dest_flat, h2, wts, g_final, yb, tm=256):
    S, D = h2.shape
    return pl.pallas_call(
        _combine_kernel,
        grid_spec=pltpu.PrefetchScalarGridSpec(
            num_scalar_prefetch=1,
            grid=(S // tm,),
            in_specs=[pl.BlockSpec((tm, D), lambda i, d: (i, 0)),
                      pl.BlockSpec((tm, LANES), lambda i, d: (i, 0)),
                      pl.BlockSpec((1, D), lambda i, d: (0, 0)),
                      pl.BlockSpec(memory_space=pl.ANY)],
            out_specs=pl.BlockSpec((tm, D), lambda i, d: (i, 0)),
            scratch_shapes=[pltpu.VMEM((2, TOP_K, tm, D), F32), pltpu.SemaphoreType.DMA((2,))]),
        out_shape=jax.ShapeDtypeStruct((S, D), F32),
        compiler_params=_cparams(("arbitrary",)),
        name="combine",
    )(dest_flat, h2, wts, g_final, yb)


def _lane_pad(v, offset):
    return jnp.zeros((1, LANES), F32).at[0, offset:offset + v.shape[0]].set(v.astype(F32))


def _layer(h, mem, g_mix, w_in, conv_w, a_log, dt_bias, g_delta_out, g_attn_out, w_out, g_cross, g_mem,
           w_q_mem, w_kv_mem, w_o_mem, g_moe, w_group, b_group, w_expert, b_expert, w_gate, w_up, w_down,
           g_final):
    S, D = h.shape
    H = N_HEADS_B
    w_main, w_small = _cast_w_in(w_in.T)
    proj, ba = _inproj(h, g_mix[None], w_main, w_small)

    slopes = jnp.asarray(2.0 ** (-8.0 * np.arange(1, N_HEADS_A + 1) / N_HEADS_A), dtype=F32)
    attn = _dilated_attention(proj, slopes)
    delta = _gated_deltanet(proj, ba, conv_w, _lane_pad(a_log, H), _lane_pad(dt_bias, H), g_delta_out[None])

    k_mem, v_mem = _memkv(mem, g_mem[None], w_kv_mem.astype(BF16))
    w_router = _split_hi_lo(jnp.pad(jnp.concatenate([w_group, w_expert], axis=1),
                                    ((0, 0), (0, LANES - N_GROUPS - N_EXPERTS))))
    b_router = _lane_pad(jnp.concatenate([b_group, b_expert]), 0)
    h2, hn, eid, wts, cnt = _mix(attn, delta, h, g_attn_out[None], w_out.astype(BF16), g_cross[None],
                                 w_q_mem.astype(BF16), k_mem, v_mem, w_o_mem.astype(BF16), g_moe[None],
                                 w_router, b_router)

    n_rows = S * TOP_K + N_EXPERTS * MOE_BLOCK
    dest, meta, tab = _rank(eid, cnt, n_rows // MOE_BLOCK)
    dest_flat = dest[:TOP_K].reshape(TOP_K * S)
    xb = _scatter_rows(dest_flat, meta[:, 1], hn, n_rows)
    yb = _experts(meta[:, 0], meta[:, 1], tab[0, :N_EXPERTS], tab[1, :N_EXPERTS], tab[2, :N_EXPERTS], xb,
                  w_gate, w_up, w_down)
    return _combine(dest_flat, h2, wts, g_final[None], yb)


def kernel(x, mem, g_mix, w_in, conv_w, a_log, dt_bias, g_delta_out, g_attn_out, w_out, g_cross, g_mem, w_q_mem,
           w_kv_mem, w_o_mem, g_moe, w_group, b_group, w_expert, b_expert, w_gate, w_up, w_down, g_final):
    assert x.shape[0] == 1 and mem.shape[0] == 1 and g_mix.shape[0] == 1
    out = _layer(x[0].astype(F32), mem[0].astype(F32), g_mix[0], w_in[0], conv_w[0], a_log[0], dt_bias[0],
                 g_delta_out[0], g_attn_out[0], w_out[0], g_cross[0], g_mem[0], w_q_mem[0], w_kv_mem[0],
                 w_o_mem[0], g_moe[0], w_group[0], b_group[0], w_expert[0], b_expert[0], w_gate[0], w_up[0],
                 w_down[0], g_final)
    return out[None].astype(x.dtype)
```

```python
import numpy as np
import jax
import jax.numpy as jnp
from jax import lax
from jax.experimental import pallas as pl
from jax.experimental.pallas import tpu as pltpu

F32 = jnp.float32
BF16 = jnp.bfloat16
I32 = jnp.int32
HIGHEST = lax.Precision.HIGHEST

EPS = 1e-6
HEAD_DIM = 128
LANES = 128
SUBLANES = 8
NEG = -1e30

N_HEADS_A = 8
N_HEADS_B = 8
W_A = N_HEADS_A * HEAD_DIM
W_B = N_HEADS_B * HEAD_DIM
W_MAIN = 3 * W_A + 4 * W_B
DILATED_PATTERNS = ((128, 1), (512, 4), (2048, 16))
ATTN_BLOCK = 128
ATTN_TILE = 2048
CONV_WIDTH = 4
DELTA_CHUNK = 64
DELTA_BLOCK = 256
N_MEM_HEADS = 4
W_MEM = N_MEM_HEADS * HEAD_DIM
N_GROUPS = 8
EXPERTS_PER_GROUP = 8
N_EXPERTS = 64
TOP_K = 2
MOE_BLOCK = 128
ROW_DMA_UNROLL = 16

VMEM_LIMIT = 56 * 1024 * 1024


def _cparams(sem):
    return pltpu.CompilerParams(dimension_semantics=sem, vmem_limit_bytes=VMEM_LIMIT)


def _rms(x, g):
    return x * lax.rsqrt(jnp.mean(x * x, axis=-1, keepdims=True) + EPS) * g


def _dot(a, b):
    return jnp.dot(a.astype(BF16), b.astype(BF16), preferred_element_type=F32)


def _dot_nt(a, b):
    return lax.dot_general(a.astype(BF16), b.astype(BF16), (((1,), (1,)), ((), ())), preferred_element_type=F32)


def _dot_tn(a, b):
    return lax.dot_general(a.astype(BF16), b.astype(BF16), (((0,), (0,)), ((), ())), preferred_element_type=F32)


def _split_hi_lo(w):
    hi = w.astype(BF16)
    lo = (w - hi.astype(F32)).astype(BF16)
    return jnp.concatenate([hi, lo], axis=1)


def _dot_x3(x, w_hl_ref):
    x_hi = x.astype(BF16)
    x_lo = (x - x_hi.astype(F32)).astype(BF16)
    r = jnp.dot(x_hi, w_hl_ref[...], preferred_element_type=F32)
    return r[:, :LANES] + r[:, LANES:] + jnp.dot(x_lo, w_hl_ref[:, :LANES], preferred_element_type=F32)


def _castw_kernel(w_ref, ws_ref, o_ref, os_ref):
    o_ref[...] = w_ref[...].astype(BF16)

    @pl.when(pl.program_id(0) == 0)
    def _():
        w = ws_ref[...]
        hi = w.astype(BF16)
        os_ref[...] = jnp.zeros_like(os_ref)
        os_ref[0:w.shape[0], :] = hi
        os_ref[LANES:LANES + w.shape[0], :] = (w - hi.astype(F32)).astype(BF16)


def _cast_w_in(w_t, tn=1024):
    D = w_t.shape[1]
    n_gate = 2 * N_HEADS_B
    assert w_t.shape[0] == W_MAIN + n_gate and W_MAIN % tn == 0 and W_MAIN % n_gate == 0
    return pl.pallas_call(
        _castw_kernel,
        grid=(W_MAIN // tn,),
        in_specs=[pl.BlockSpec((tn, D), lambda j: (j, 0)),
                  pl.BlockSpec((n_gate, D), lambda j: (W_MAIN // n_gate, 0))],
        out_specs=[pl.BlockSpec((tn, D), lambda j: (j, 0)),
                   pl.BlockSpec((2 * LANES, D), lambda j: (0, 0))],
        out_shape=[jax.ShapeDtypeStruct((W_MAIN, D), BF16), jax.ShapeDtypeStruct((2 * LANES, D), BF16)],
        compiler_params=_cparams(("arbitrary",)),
        name="castw",
    )(w_t, w_t)


def _inproj_kernel(x_ref, g_ref, w_ref, ws_ref, o_ref, os_ref, u_scr):
    @pl.when(pl.program_id(1) == 0)
    def _():
        u = _rms(x_ref[...], g_ref[...])
        u_hi = u.astype(BF16)
        u_scr[...] = u_hi
        u_lo = (u - u_hi.astype(F32)).astype(BF16)
        r = _dot_nt(u_hi, ws_ref[...])
        os_ref[...] = r[:, :LANES] + r[:, LANES:] + _dot_nt(u_lo, ws_ref[0:LANES, :])

    o_ref[...] = _dot_nt(u_scr[...], w_ref[...])


def _inproj(x, g, w_main, w_small, tm=1024, tn=1024):
    S, D = x.shape
    N = w_main.shape[0]
    return pl.pallas_call(
        _inproj_kernel,
        grid=(S // tm, N // tn),
        in_specs=[pl.BlockSpec((tm, D), lambda i, j: (i, 0)),
                  pl.BlockSpec((1, D), lambda i, j: (0, 0)),
                  pl.BlockSpec((tn, D), lambda i, j: (j, 0)),
                  pl.BlockSpec((2 * LANES, D), lambda i, j: (0, 0))],
        out_specs=[pl.BlockSpec((tm, tn), lambda i, j: (i, j)),
                   pl.BlockSpec((tm, LANES), lambda i, j: (i, 0))],
        out_shape=[jax.ShapeDtypeStruct((S, N), F32), jax.ShapeDtypeStruct((S, LANES), F32)],
        scratch_shapes=[pltpu.VMEM((tm, D), BF16)],
        compiler_params=_cparams(("parallel", "arbitrary")),
        name="inproj",
    )(x, g, w_main, w_small)


def _attn_kernel(slope_ref, q_ref, kp_ref, kc_ref, vp_ref, vc_ref, o_ref, kk, vv, o_scr, l_scr):
    i = pl.program_id(0)
    h = pl.program_id(1)
    T = ATTN_TILE
    B = ATTN_BLOCK
    slope = slope_ref[h]
    kk[0:T, :] = kp_ref[...]
    kk[T:2 * T, :] = kc_ref[...]
    vv[0:T, :] = vp_ref[...]
    vv[T:2 * T, :] = vc_ref[...]
    qi = lax.broadcasted_iota(I32, (B, 2 * B), 0)
    ki = lax.broadcasted_iota(I32, (B, 2 * B), 1)
    dist = qi + B - ki
    band = (dist >= 0) & (dist <= B)
    band_first = band & (ki >= B)
    distf = dist.astype(F32)
    log2e = float(np.log2(np.e))
    scale = HEAD_DIM ** -0.5 * log2e

    for p, (window, d) in enumerate(DILATED_PATTERNS):
        assert window // d == B and T % (B * d) == 0
        bias = jnp.where(band, distf * (-slope * d * log2e), NEG)
        bias_first = jnp.where(band_first, distf * (-slope * d * log2e), NEG)
        bias_j0 = jnp.where(i == 0, bias_first, bias)

        for b in range(T // B):
            r = b % d
            j = b // d
            qs = r + B * d * j
            ks = T - B * d + qs
            q = q_ref[pl.ds(qs, B, stride=d), :] * scale
            k = kk[pl.ds(ks, 2 * B, stride=d), :]
            v = vv[pl.ds(ks, 2 * B, stride=d), :]
            s = _dot_nt(q, k) + (bias_j0 if j == 0 else bias)
            m = jnp.max(s, axis=-1, keepdims=True)
            e = jnp.exp2(s - m)
            l = jnp.sum(e, axis=-1, keepdims=True)
            o = _dot(e, v) / l
            lse = m + jnp.log2(l)
            o_scr[p, pl.ds(qs, B, stride=d), :] = o
            l_scr[p, pl.ds(qs, B, stride=d), :] = jnp.broadcast_to(lse, (B, HEAD_DIM))

    l0, l1, l2 = l_scr[0], l_scr[1], l_scr[2]
    m = jnp.maximum(jnp.maximum(l0, l1), l2)
    w0, w1, w2 = jnp.exp2(l0 - m), jnp.exp2(l1 - m), jnp.exp2(l2 - m)
    o_ref[...] = (w0 * o_scr[0] + w1 * o_scr[1] + w2 * o_scr[2]) / (w0 + w1 + w2)


def _dilated_attention(proj, slopes):
    S = proj.shape[0]
    T = ATTN_TILE
    H = N_HEADS_A
    blk = (T, HEAD_DIM)
    return pl.pallas_call(
        _attn_kernel,
        grid_spec=pltpu.PrefetchScalarGridSpec(
            num_scalar_prefetch=1,
            grid=(S // T, H),
            in_specs=[pl.BlockSpec(blk, lambda i, h, s: (i, h)),
                      pl.BlockSpec(blk, lambda i, h, s: (jnp.maximum(i - 1, 0), H + h)),
                      pl.BlockSpec(blk, lambda i, h, s: (i, H + h)),
                      pl.BlockSpec(blk, lambda i, h, s: (jnp.maximum(i - 1, 0), 2 * H + h)),
                      pl.BlockSpec(blk, lambda i, h, s: (i, 2 * H + h))],
            out_specs=pl.BlockSpec(blk, lambda i, h, s: (i, h)),
            scratch_shapes=[pltpu.VMEM((2 * T, HEAD_DIM), F32), pltpu.VMEM((2 * T, HEAD_DIM), F32),
                            pltpu.VMEM((3, T, HEAD_DIM), F32), pltpu.VMEM((3, T, HEAD_DIM), F32)]),
        out_shape=jax.ShapeDtypeStruct((S, W_A), F32),
        compiler_params=_cparams(("parallel", "parallel")),
        name="attn",
    )(slopes, proj, proj, proj, proj, proj)


def _softplus(x):
    return jnp.maximum(x, 0.0) + jnp.log1p(jnp.exp(-jnp.abs(x)))


def _delta_kernel(hist_ref, qkv_ref, z_ref, ba_ref, cw_ref, alog_ref, dtb_ref, gout_ref, o_ref,
                  ext, act, gct, state):
    i = pl.program_id(0)
    CB = DELTA_BLOCK
    C = DELTA_CHUNK
    H = N_HEADS_B
    D = HEAD_DIM

    @pl.when(i == 0)
    def _():
        state[...] = jnp.zeros_like(state)
        ext[0:SUBLANES, :] = jnp.zeros((SUBLANES, 3 * W_B), F32)

    @pl.when(i > 0)
    def _():
        ext[0:SUBLANES, :] = hist_ref[...]

    ext[SUBLANES:SUBLANES + CB, :] = qkv_ref[...]

    for c in range(3 * H):
        sl = slice(c * D, (c + 1) * D)
        acc = cw_ref[CONV_WIDTH - 1:CONV_WIDTH, sl] * ext[SUBLANES:SUBLANES + CB, sl]
        for t in range(1, CONV_WIDTH):
            acc = acc + cw_ref[CONV_WIDTH - 1 - t:CONV_WIDTH - t, sl] * ext[SUBLANES - t:SUBLANES - t + CB, sl]
        a = acc * jax.nn.sigmoid(acc)
        if c < 2 * H:
            a = a * lax.rsqrt(jnp.sum(a * a, axis=-1, keepdims=True) + EPS)
        if c < H:
            a = a * (D ** -0.5)
        act[:, sl] = a

    ba = ba_ref[...]
    beta = jax.nn.sigmoid(ba)
    g = -jnp.exp(alog_ref[...]) * _softplus(ba + dtb_ref[...])
    row = lax.broadcasted_iota(I32, (CB, CB), 0)
    col = lax.broadcasted_iota(I32, (CB, CB), 1)
    chunk_tri = jnp.where((row // C == col // C) & (row >= col), 1.0, 0.0).astype(F32)
    gc = jnp.dot(chunk_tri, g, precision=HIGHEST, preferred_element_type=F32)
    gct[...] = gc.T
    eg = jnp.exp(gc)

    ri = lax.broadcasted_iota(I32, (C, C), 0)
    ci = lax.broadcasted_iota(I32, (C, C), 1)
    causal = ri >= ci
    strict = ri > ci
    gout = gout_ref[...]

    heads = range(H)
    for n in range(CB // C):
        rows = slice(n * C, (n + 1) * C)
        q = [act[rows, h * D:(h + 1) * D] for h in heads]
        k = [act[rows, (H + h) * D:(H + h + 1) * D] for h in heads]
        v = [act[rows, (2 * H + h) * D:(2 * H + h + 1) * D] for h in heads]
        beta_c = [beta[rows, h:h + 1] for h in heads]
        gc_c = [gc[rows, H + h:H + h + 1] for h in heads]
        eg_c = [eg[rows, H + h:H + h + 1] for h in heads]
        gc_r = [gct[H + h:H + h + 1, rows] for h in heads]
        g_last = [gc_r[h][:, C - 1:C] for h in heads]
        decay = [jnp.exp(jnp.where(causal, gc_c[h] - gc_r[h], NEG)) for h in heads]
        kb = [k[h] * beta_c[h] for h in heads]
        kq = [_dot_nt(jnp.concatenate([kb[h], q[h]], axis=0), k[h]) for h in heads]
        qk = [jnp.where(causal, kq[h][C:] * decay[h], 0.0) for h in heads]
        pw = [jnp.where(strict, -kq[h][:C] * decay[h], 0.0) for h in heads]
        r = pw
        pw = [_dot(pw[h], pw[h]) for h in heads]
        for _ in range(int(np.log2(C)) - 2):
            pr = [_dot(jnp.concatenate([pw[h], r[h]], axis=0), pw[h]) for h in heads]
            r = [r[h] + pw[h] + pr[h][C:] for h in heads]
            pw = [pr[h][:C] for h in heads]
        r = [r[h] + pw[h] + _dot(r[h], pw[h]) for h in heads]
        bmat = [jnp.concatenate([v[h] * beta_c[h], kb[h] * eg_c[h]], axis=-1) for h in heads]
        uw = [bmat[h] + _dot(r[h], bmat[h]) for h in heads]
        wq = [jnp.concatenate([uw[h][:, D:], q[h] * eg_c[h]], axis=0) for h in heads]
        kd = [k[h] * jnp.exp(g_last[h] - gc_c[h]) for h in heads]
        s_old = [state[h] for h in heads]
        ws = [_dot(wq[h], s_old[h]) for h in heads]
        v_new = [uw[h][:, :D] - ws[h][:C] for h in heads]
        for h in heads:
            state[h] = s_old[h] * jnp.exp(g_last[h]) + _dot_tn(kd[h], v_new[h])
        o = [ws[h][C:] + _dot(qk[h], v_new[h]) for h in heads]
        for h in heads:
            z = z_ref[rows, h * D:(h + 1) * D]
            o_ref[rows, h * D:(h + 1) * D] = _rms(o[h], gout) * (z * jax.nn.sigmoid(z))


def _gated_deltanet(proj, ba, conv_w, alog_pad, dtb_pad, g_out):
    S = proj.shape[0]
    CB = DELTA_BLOCK
    qkv_col = W_A * 3 // (3 * W_B)
    z_col = (3 * W_A + 3 * W_B) // W_B
    return pl.pallas_call(
        _delta_kernel,
        grid=(S // CB,),
        in_specs=[pl.BlockSpec((SUBLANES, 3 * W_B), lambda i: (jnp.maximum(i * (CB // SUBLANES) - 1, 0), qkv_col)),
                  pl.BlockSpec((CB, 3 * W_B), lambda i: (i, qkv_col)),
                  pl.BlockSpec((CB, W_B), lambda i: (i, z_col)),
                  pl.BlockSpec((CB, LANES), lambda i: (i, 0)),
                  pl.BlockSpec((CONV_WIDTH, 3 * W_B), lambda i: (0, 0)),
                  pl.BlockSpec((1, LANES), lambda i: (0, 0)),
                  pl.BlockSpec((1, LANES), lambda i: (0, 0)),
                  pl.BlockSpec((1, HEAD_DIM), lambda i: (0, 0))],
        out_specs=pl.BlockSpec((CB, W_B), lambda i: (i, 0)),
        out_shape=jax.ShapeDtypeStruct((S, W_B), F32),
        scratch_shapes=[pltpu.VMEM((CB + SUBLANES, 3 * W_B), F32),
                        pltpu.VMEM((CB, 3 * W_B), F32),
                        pltpu.VMEM((LANES, CB), F32),
                        pltpu.VMEM((N_HEADS_B, HEAD_DIM, HEAD_DIM), F32)],
        compiler_params=_cparams(("arbitrary",)),
        name="delta",
    )(proj, proj, proj, ba, conv_w, alog_pad, dtb_pad, g_out)


def _memkv_kernel(mem_ref, g_ref, w_ref, k_ref, v_ref):
    kv = _dot(_rms(mem_ref[...], g_ref[...]), w_ref[...])
    k_ref[...] = kv[:, :W_MEM].astype(BF16)
    v_ref[...] = kv[:, W_MEM:].astype(BF16)


def _memkv(mem, g, w_kv):
    M = mem.shape[0]
    return pl.pallas_call(
        _memkv_kernel,
        out_shape=[jax.ShapeDtypeStruct((M, W_MEM), BF16), jax.ShapeDtypeStruct((M, W_MEM), BF16)],
        compiler_params=pltpu.CompilerParams(vmem_limit_bytes=VMEM_LIMIT),
        name="memkv",
    )(mem, g, w_kv)


def _mix_kernel(attn_ref, delta_ref, x_ref, gattn_ref, wout_ref, gcross_ref, wq_ref, km_ref, vm_ref, wo_ref,
                gmoe_ref, wr_ref, br_ref, h2_ref, hn_ref, eid_ref, wts_ref, cnt_ref):
    an = _rms(attn_ref[...], gattn_ref[...])
    mix = jnp.concatenate([an.astype(BF16), delta_ref[...].astype(BF16)], axis=-1)
    h1 = x_ref[...] + jnp.dot(mix, wout_ref[...], preferred_element_type=F32)

    q = _dot(_rms(h1, gcross_ref[...]), wq_ref[...]) * (HEAD_DIM ** -0.5)
    outs = []
    for hh in range(N_MEM_HEADS):
        sl = slice(hh * HEAD_DIM, (hh + 1) * HEAD_DIM)
        s = _dot_nt(q[:, sl], km_ref[:, sl])
        e = jnp.exp(s - jnp.max(s, axis=-1, keepdims=True))
        outs.append(_dot(e, vm_ref[:, sl]) / jnp.sum(e, axis=-1, keepdims=True))
    h2 = h1 + _dot(jnp.concatenate(outs, axis=-1), wo_ref[...])
    h2_ref[...] = h2

    hn = _rms(h2, gmoe_ref[...])
    hn_ref[...] = hn
    logits = _dot_x3(hn, wr_ref) + br_ref[...]
    lane = lax.broadcasted_iota(I32, logits.shape, 1)
    gl = jnp.where(lane < N_GROUPS, logits, NEG)
    gmax = jnp.max(gl, axis=-1, keepdims=True)
    g_sel = jnp.min(jnp.where(gl == gmax, lane, LANES), axis=-1, keepdims=True)
    g_gate = 1.0 / jnp.sum(jnp.exp(gl - gmax), axis=-1, keepdims=True)
    in_group = (lane >= N_GROUPS) & ((lane - N_GROUPS) // EXPERTS_PER_GROUP == g_sel)
    el = jnp.where(in_group, logits, NEG)
    v1 = jnp.max(el, axis=-1, keepdims=True)
    i1 = jnp.min(jnp.where(in_group & (el == v1), lane, LANES), axis=-1, keepdims=True)
    in_rest = in_group & (lane != i1)
    el2 = jnp.where(in_rest, logits, NEG)
    v2 = jnp.max(el2, axis=-1, keepdims=True)
    i2 = jnp.min(jnp.where(in_rest & (el2 == v2), lane, LANES), axis=-1, keepdims=True)
    e2 = jnp.exp(v2 - v1)
    w1 = g_gate / (1.0 + e2)
    w2 = g_gate * e2 / (1.0 + e2)
    eid_ref[...] = jnp.where(lane == 0, i1 - N_GROUPS, jnp.where(lane == 1, i2 - N_GROUPS, 0))
    wts_ref[...] = jnp.where(lane == 0, w1, jnp.where(lane == 1, w2, 0.0))

    @pl.when(pl.program_id(0) == 0)
    def _():
        cnt_ref[...] = jnp.zeros_like(cnt_ref)

    picked = (lane == i1 - N_GROUPS) | (lane == i2 - N_GROUPS)
    cnt_ref[...] = cnt_ref[...] + jnp.sum(jnp.where(picked, 1.0, 0.0), axis=0, keepdims=True)


def _mix(attn, delta, x, g_attn, w_out, g_cross, w_q, k_mem, v_mem, w_o, g_moe, w_router, b_router, tm=512):
    S, D = x.shape
    row = lambda w: pl.BlockSpec((tm, w), lambda i: (i, 0))
    full = lambda a: pl.BlockSpec(a.shape, lambda i: (0, 0), pipeline_mode=pl.Buffered(1))
    return pl.pallas_call(
        _mix_kernel,
        grid=(S // tm,),
        in_specs=[row(W_A), row(W_B), row(D), full(g_attn), full(w_out), full(g_cross), full(w_q),
                  full(k_mem), full(v_mem), full(w_o), full(g_moe), full(w_router), full(b_router)],
        out_specs=[row(D), row(D), row(LANES), row(LANES), pl.BlockSpec((SUBLANES, LANES), lambda i: (0, 0))],
        out_shape=[jax.ShapeDtypeStruct((S, D), F32), jax.ShapeDtypeStruct((S, D), F32),
                   jax.ShapeDtypeStruct((S, LANES), I32), jax.ShapeDtypeStruct((S, LANES), F32),
                   jax.ShapeDtypeStruct((SUBLANES, LANES), F32)],
        compiler_params=_cparams(("arbitrary",)),
        name="mix",
    )(attn, delta, x, g_attn, w_out, g_cross, w_q, k_mem, v_mem, w_o, g_moe, w_router, b_router)


def _rank_kernel(eid_ref, cnt_ref, dest_ref, meta_ref, tab_ref, carry, pstart):
    i = pl.program_id(0)
    tm = eid_ref.shape[0]
    lane = lax.broadcasted_iota(I32, (tm, LANES), 1)
    e = eid_ref[...]
    e1 = e[:, 0:1]
    e2 = e[:, 1:2]
    onehot = jnp.where((lane == e1) | (lane == e2), 1.0, 0.0).astype(F32)

    @pl.when(i == 0)
    def _():
        cnt = cnt_ref[...]
        lane8 = lax.broadcasted_iota(I32, cnt.shape, 1)
        is_expert = lane8 < N_EXPERTS

        def prefix_sum(x):
            shift = 1
            while shift < LANES:
                x = x + jnp.where(lane8 >= shift, pltpu.roll(x, shift, 1), 0.0)
                shift *= 2
            return x

        padded = jnp.floor((cnt + (MOE_BLOCK - 1)) * (1.0 / MOE_BLOCK)) * MOE_BLOCK
        pend = prefix_sum(padded)
        pstart[...] = pend - padded
        carry[...] = jnp.zeros_like(carry)

        nb = meta_ref.shape[0]
        blane = lax.broadcasted_iota(I32, (nb, LANES), 1)
        brow = (lax.broadcasted_iota(I32, (nb, LANES), 0) * MOE_BLOCK).astype(F32)
        owner = jnp.sum(jnp.where((blane < N_EXPERTS) & (brow >= pend[0:1, :]), 1.0, 0.0), axis=-1, keepdims=True)
        owner = jnp.minimum(owner, N_EXPERTS - 1.0)
        seg_end = jnp.sum(jnp.where(blane.astype(F32) == owner, (pend - padded + cnt)[0:1, :], 0.0),
                          axis=-1, keepdims=True)
        valid = jnp.clip(seg_end - brow, 0.0, MOE_BLOCK * 1.0)
        meta_ref[...] = jnp.where(blane == 0, owner, jnp.where(blane == 1, valid, 0.0)).astype(I32)

        present = is_expert & (cnt > 0.0)
        order = prefix_sum(jnp.where(present, 1.0, 0.0)) - 1.0
        parity = order - 2.0 * jnp.floor(order * 0.5)
        nxt = jnp.where(present, lane8.astype(F32), N_EXPERTS * 1.0)
        nxt = jnp.where(lane8 < LANES - 1, pltpu.roll(nxt, LANES - 1, 1), N_EXPERTS * 1.0)
        shift = 1
        while shift < LANES:
            nxt = jnp.minimum(nxt, jnp.where(lane8 < LANES - shift, pltpu.roll(nxt, LANES - shift, 1), N_EXPERTS * 1.0))
            shift *= 2
        nxt = jnp.where(nxt >= N_EXPERTS, -1.0, nxt)
        sub8 = lax.broadcasted_iota(I32, cnt.shape, 0)
        first_block = (pend - padded) * (1.0 / MOE_BLOCK)
        rows = (parity, nxt, first_block)
        tab = jnp.zeros_like(cnt)
        for k, v in enumerate(rows):
            tab = jnp.where(sub8 == k, v, tab)
        tab_ref[...] = tab.astype(I32)

    r = lax.broadcasted_iota(I32, (tm, tm), 0)
    c = lax.broadcasted_iota(I32, (tm, tm), 1)
    before = jnp.where(r > c, 1.0, 0.0).astype(BF16)
    pos = jnp.dot(before, onehot.astype(BF16), preferred_element_type=F32) + carry[0:1, :] + pstart[0:1, :]
    d1 = jnp.sum(jnp.where(lane == e1, pos, 0.0), axis=-1, keepdims=True)
    d2 = jnp.sum(jnp.where(lane == e2, pos, 0.0), axis=-1, keepdims=True)
    dest = jnp.where(lane == 0, d1, jnp.where(lane == 1, d2, 0.0))
    dest_ref[...] = dest.T[:SUBLANES].astype(I32)
    carry[...] = carry[...] + jnp.sum(onehot, axis=0, keepdims=True)


def _rank(eid, cnt, n_blocks, tm=256):
    S = eid.shape[0]
    const = lambda shape: pl.BlockSpec(shape, lambda i: (0, 0))
    return pl.pallas_call(
        _rank_kernel,
        grid=(S // tm,),
        in_specs=[pl.BlockSpec((tm, LANES), lambda i: (i, 0)), const((SUBLANES, LANES))],
        out_specs=[pl.BlockSpec((SUBLANES, tm), lambda i: (0, i)), const((n_blocks, LANES)), const((SUBLANES, LANES))],
        out_shape=[jax.ShapeDtypeStruct((SUBLANES, S), I32), jax.ShapeDtypeStruct((n_blocks, LANES), I32),
                   jax.ShapeDtypeStruct((SUBLANES, LANES), I32)],
        scratch_shapes=[pltpu.VMEM((SUBLANES, LANES), F32), pltpu.VMEM((SUBLANES, LANES), F32)],
        compiler_params=_cparams(("arbitrary",)),
        name="rank",
    )(eid, cnt)


def _scatter_kernel(dest_ref, nval_ref, hn_ref, xb_ref, zeros, sem):
    i = pl.program_id(0)
    tm = hn_ref.shape[0]

    @pl.when(i == 0)
    def _():
        zeros[...] = jnp.zeros_like(zeros)

        def zero_block(b):
            return pltpu.make_async_copy(zeros, xb_ref.at[pl.ds(b * MOE_BLOCK, MOE_BLOCK), :], sem)

        def zissue(b, carry):
            @pl.when(nval_ref[b] < MOE_BLOCK)
            def _():
                zero_block(b).start()
            return carry

        def zdrain(b, carry):
            @pl.when(nval_ref[b] < MOE_BLOCK)
            def _():
                zero_block(b).wait()
            return carry

        lax.fori_loop(0, nval_ref.shape[0], zissue, 0)
        lax.fori_loop(0, nval_ref.shape[0], zdrain, 0)

    n_tok = pl.num_programs(0) * tm

    def issue(r, carry):
        for k in range(TOP_K):
            d = dest_ref[k * n_tok + i * tm + r]
            pltpu.make_async_copy(hn_ref.at[pl.ds(r, 1), :], xb_ref.at[pl.ds(d, 1), :], sem).start()
        return carry

    lax.fori_loop(0, tm, issue, 0, unroll=ROW_DMA_UNROLL)
    for k in range(TOP_K):
        pltpu.make_async_copy(hn_ref, xb_ref.at[pl.ds(0, tm), :], sem).wait()


def _scatter_rows(dest_flat, block_valid, hn, n_rows, tm=512):
    S, D = hn.shape
    return pl.pallas_call(
        _scatter_kernel,
        grid_spec=pltpu.PrefetchScalarGridSpec(
            num_scalar_prefetch=2,
            grid=(S // tm,),
            in_specs=[pl.BlockSpec((tm, D), lambda i, d, nv: (i, 0))],
            out_specs=pl.BlockSpec(memory_space=pl.ANY),
            scratch_shapes=[pltpu.VMEM((MOE_BLOCK, D), F32), pltpu.SemaphoreType.DMA(())]),
        out_shape=jax.ShapeDtypeStruct((n_rows, D), F32),
        compiler_params=_cparams(("arbitrary",)),
        name="scatter",
    )(dest_flat, block_valid, hn)


def _expert_kernel(bexp_ref, nval_ref, slot_ref, next_ref, first_ref, x_ref, wg_ref, wu_ref, wd_ref, y_ref,
                   wg_f, wu_f, wd_f, wg_s, wu_s, wd_s, sem):
    b = pl.program_id(0)
    nb = pl.num_programs(0)
    e = bexp_ref[b]
    e_prev = bexp_ref[jnp.maximum(b - 1, 0)]
    nval = nval_ref[b]

    def weight_copy(w, ex, slot):
        src, dst = ((wg_ref, wg_f), (wu_ref, wu_f), (wd_ref, wd_f))[w]
        return pltpu.make_async_copy(src.at[ex], dst.at[slot], sem.at[slot, w])

    @pl.when(jnp.logical_and(b == 0, nval > 0))
    def _():
        for w in range(3):
            weight_copy(w, e, slot_ref[e]).start()

    @pl.when(jnp.logical_and(nval > 0, jnp.logical_or(b == 0, e != e_prev)))
    def _():
        slot = slot_ref[e]
        for w in range(3):
            weight_copy(w, e, slot).wait()
        wg_s[...] = wg_f[slot].astype(BF16)
        wu_s[...] = wu_f[slot].astype(BF16)
        wd_s[...] = wd_f[slot].astype(BF16)

    @pl.when(nval == 0)
    def _():
        y_ref[...] = jnp.zeros_like(y_ref)

    @pl.when(nval > 0)
    def _():
        nxt = next_ref[e]
        j = b - first_ref[e]
        b1 = jnp.minimum(b + 1, nb - 1)
        last = jnp.logical_or(b + 1 == nb, jnp.logical_or(bexp_ref[b1] != e, nval_ref[b1] == 0))
        for w in range(3):
            @pl.when(jnp.logical_and(nxt >= 0, jnp.logical_or(j == w, jnp.logical_and(last, j < w))))
            def _(w=w):
                weight_copy(w, nxt, 1 - slot_ref[e]).start()

        x = x_ref[...].astype(BF16)
        gate = jnp.dot(x, wg_s[...], preferred_element_type=F32)
        up = jnp.dot(x, wu_s[...], preferred_element_type=F32)
        hid = gate * jax.nn.sigmoid(gate) * up
        y_ref[...] = jnp.dot(hid.astype(BF16), wd_s[...], preferred_element_type=F32)


def _experts(block_expert, block_valid, expert_slot, expert_next, expert_first, xb, w_gate, w_up, w_down):
    P, D = xb.shape
    DE = w_gate.shape[-1]
    row_blk = pl.BlockSpec((MOE_BLOCK, D), lambda b, *_: (b, 0))
    hbm = pl.BlockSpec(memory_space=pl.ANY)
    return pl.pallas_call(
        _expert_kernel,
        grid_spec=pltpu.PrefetchScalarGridSpec(
            num_scalar_prefetch=5,
            grid=(P // MOE_BLOCK,),
            in_specs=[row_blk, hbm, hbm, hbm],
            out_specs=row_blk,
            scratch_shapes=[pltpu.VMEM((2, D, DE), F32), pltpu.VMEM((2, D, DE), F32), pltpu.VMEM((2, DE, D), F32),
                            pltpu.VMEM((D, DE), BF16), pltpu.VMEM((D, DE), BF16), pltpu.VMEM((DE, D), BF16),
                            pltpu.SemaphoreType.DMA((2, 3))]),
        out_shape=jax.ShapeDtypeStruct((P, D), F32),
        compiler_params=_cparams(("arbitrary",)),
        name="experts",
    )(block_expert, block_valid, expert_slot, expert_next, expert_first, xb, w_gate, w_up, w_down)


def _combine_kernel(dest_ref, h2_ref, wts_ref, g_ref, yb_ref, o_ref, buf, sem):
    i = pl.program_id(0)
    n = pl.num_programs(0)
    tm = h2_ref.shape[0]
    n_tok = n * tm

    def gather_block(blk, slot):
        def issue(r, carry):
            for k in range(TOP_K):
                d = dest_ref[k * n_tok + blk * tm + r]
                pltpu.make_async_copy(yb_ref.at[pl.ds(d, 1), :], buf.at[slot, k, pl.ds(r, 1), :], sem.at[slot]).start()
            return carry

        lax.fori_loop(0, tm, issue, 0, unroll=ROW_DMA_UNROLL)

    @pl.when(i == 0)
    def _():
        gather_block(0, 0)

    @pl.when(i + 1 < n)
    def _():
        gather_block(i + 1, (i + 1) % 2)

    slot = i % 2
    for k in range(TOP_K):
        pltpu.make_async_copy(yb_ref.at[pl.ds(0, tm), :], buf.at[slot, k], sem.at[slot]).wait()
    wts = wts_ref[...]
    h = h2_ref[...] + wts[:, 0:1] * buf[slot, 0] + wts[:, 1:2] * buf[slot, 1]
    o_ref[...] = _rms(h, g_ref[...])


def _combine(dest_flat, h2, wts, g_final, yb, tm=256):
    S, D = h2.shape
    return pl.pallas_call(
        _combine_kernel,
        grid_spec=pltpu.PrefetchScalarGridSpec(
            num_scalar_prefetch=1,
            grid=(S // tm,),
            in_specs=[pl.BlockSpec((tm, D), lambda i, d: (i, 0)),
                      pl.BlockSpec((tm, LANES), lambda i, d: (i, 0)),
                      pl.BlockSpec((1, D), lambda i, d: (0, 0)),
                      pl.BlockSpec(memory_space=pl.ANY)],
            out_specs=pl.BlockSpec((tm, D), lambda i, d: (i, 0)),
            scratch_shapes=[pltpu.VMEM((2, TOP_K, tm, D), F32), pltpu.SemaphoreType.DMA((2,))]),
        out_shape=jax.ShapeDtypeStruct((S, D), F32),
        compiler_params=_cparams(("arbitrary",)),
        name="combine",
    )(dest_flat, h2, wts, g_final, yb)


def _lane_pad(v, offset):
    return jnp.zeros((1, LANES), F32).at[0, offset:offset + v.shape[0]].set(v.astype(F32))


def _layer(h, mem, g_mix, w_in, conv_w, a_log, dt_bias, g_delta_out, g_attn_out, w_out, g_cross, g_mem,
           w_q_mem, w_kv_mem, w_o_mem, g_moe, w_group, b_group, w_expert, b_expert, w_gate, w_up, w_down,
           g_final):
    S, D = h.shape
    H = N_HEADS_B
    w_main, w_small = _cast_w_in(w_in.T)
    proj, ba = _inproj(h, g_mix[None], w_main, w_small)

    slopes = jnp.asarray(2.0 ** (-8.0 * np.arange(1, N_HEADS_A + 1) / N_HEADS_A), dtype=F32)
    attn = _dilated_attention(proj, slopes)
    delta = _gated_deltanet(proj, ba, conv_w, _lane_pad(a_log, H), _lane_pad(dt_bias, H), g_delta_out[None])

    k_mem, v_mem = _memkv(mem, g_mem[None], w_kv_mem.astype(BF16))
    w_router = _split_hi_lo(jnp.pad(jnp.concatenate([w_group, w_expert], axis=1),
                                    ((0, 0), (0, LANES - N_GROUPS - N_EXPERTS))))
    b_router = _lane_pad(jnp.concatenate([b_group, b_expert]), 0)
    h2, hn, eid, wts, cnt = _mix(attn, delta, h, g_attn_out[None], w_out.astype(BF16), g_cross[None],
                                 w_q_mem.astype(BF16), k_mem, v_mem, w_o_mem.astype(BF16), g_moe[None],
                                 w_router, b_router)

    n_rows = S * TOP_K + N_EXPERTS * MOE_BLOCK
    dest, meta, tab = _rank(eid, cnt, n_rows // MOE_BLOCK)
    dest_flat = dest[:TOP_K].reshape(TOP_K * S)
    xb = _scatter_rows(dest_flat, meta[:, 1], hn, n_rows)
    yb = _experts(meta[:, 0], meta[:, 1], tab[0, :N_EXPERTS], tab[1, :N_EXPERTS], tab[2, :N_EXPERTS], xb,
                  w_gate, w_up, w_down)
    return _combine(dest_flat, h2, wts, g_final[None], yb)


def kernel(x, mem, g_mix, w_in, conv_w, a_log, dt_bias, g_delta_out, g_attn_out, w_out, g_cross, g_mem, w_q_mem,
           w_kv_mem, w_o_mem, g_moe, w_group, b_group, w_expert, b_expert, w_gate, w_up, w_down, g_final):
    assert x.shape[0] == 1 and mem.shape[0] == 1 and g_mix.shape[0] == 1
    out = _layer(x[0].astype(F32), mem[0].astype(F32), g_mix[0], w_in[0], conv_w[0], a_log[0], dt_bias[0],
                 g_delta_out[0], g_attn_out[0], w_out[0], g_cross[0], g_mem[0], w_q_mem[0], w_kv_mem[0],
                 w_o_mem[0], g_moe[0], w_group[0], b_group[0], w_expert[0], b_expert[0], w_gate[0], w_up[0],
                 w_down[0], g_final)
    return out[None].astype(x.dtype)
```

```python
import numpy as np
import jax
import jax.numpy as jnp
from jax import lax
from jax.experimental import pallas as pl
from jax.experimental.pallas import tpu as pltpu

F32 = jnp.float32
BF16 = jnp.bfloat16
I32 = jnp.int32
HIGHEST = lax.Precision.HIGHEST

EPS = 1e-6
HEAD_DIM = 128
LANES = 128
SUBLANES = 8
NEG = -1e30

N_HEADS_A = 8
N_HEADS_B = 8
W_A = N_HEADS_A * HEAD_DIM
W_B = N_HEADS_B * HEAD_DIM
W_MAIN = 3 * W_A + 4 * W_B
DILATED_PATTERNS = ((128, 1), (512, 4), (2048, 16))
ATTN_BLOCK = 128
ATTN_TILE = 2048
CONV_WIDTH = 4
DELTA_CHUNK = 64
DELTA_BLOCK = 256
N_MEM_HEADS = 4
W_MEM = N_MEM_HEADS * HEAD_DIM
N_GROUPS = 8
EXPERTS_PER_GROUP = 8
N_EXPERTS = 64
TOP_K = 2
MOE_BLOCK = 128
ROW_DMA_UNROLL = 16

VMEM_LIMIT = 56 * 1024 * 1024


def _cparams(sem):
    return pltpu.CompilerParams(dimension_semantics=sem, vmem_limit_bytes=VMEM_LIMIT)


def _rms(x, g):
    return x * lax.rsqrt(jnp.mean(x * x, axis=-1, keepdims=True) + EPS) * g


def _dot(a, b):
    return jnp.dot(a.astype(BF16), b.astype(BF16), preferred_element_type=F32)


def _dot_nt(a, b):
    return lax.dot_general(a.astype(BF16), b.astype(BF16), (((1,), (1,)), ((), ())), preferred_element_type=F32)


def _dot_tn(a, b):
    return lax.dot_general(a.astype(BF16), b.astype(BF16), (((0,), (0,)), ((), ())), preferred_element_type=F32)


def _split_hi_lo(w):
    hi = w.astype(BF16)
    lo = (w - hi.astype(F32)).astype(BF16)
    return jnp.concatenate([hi, lo], axis=1)


def _dot_x3(x, w_hl_ref):
    x_hi = x.astype(BF16)
    x_lo = (x - x_hi.astype(F32)).astype(BF16)
    r = jnp.dot(x_hi, w_hl_ref[...], preferred_element_type=F32)
    return r[:, :LANES] + r[:, LANES:] + jnp.dot(x_lo, w_hl_ref[:, :LANES], preferred_element_type=F32)


def _castw_kernel(w_ref, ws_ref, o_ref, os_ref):
    o_ref[...] = w_ref[...].astype(BF16)

    @pl.when(pl.program_id(0) == 0)
    def _():
        w = ws_ref[...]
        hi = w.astype(BF16)
        os_ref[...] = jnp.zeros_like(os_ref)
        os_ref[0:w.shape[0], :] = hi
        os_ref[LANES:LANES + w.shape[0], :] = (w - hi.astype(F32)).astype(BF16)


def _cast_w_in(w_t, tn=1024):
    D = w_t.shape[1]
    n_gate = 2 * N_HEADS_B
    assert w_t.shape[0] == W_MAIN + n_gate and W_MAIN % tn == 0 and W_MAIN % n_gate == 0
    return pl.pallas_call(
        _castw_kernel,
        grid=(W_MAIN // tn,),
        in_specs=[pl.BlockSpec((tn, D), lambda j: (j, 0)),
                  pl.BlockSpec((n_gate, D), lambda j: (W_MAIN // n_gate, 0))],
        out_specs=[pl.BlockSpec((tn, D), lambda j: (j, 0)),
                   pl.BlockSpec((2 * LANES, D), lambda j: (0, 0))],
        out_shape=[jax.ShapeDtypeStruct((W_MAIN, D), BF16), jax.ShapeDtypeStruct((2 * LANES, D), BF16)],
        compiler_params=_cparams(("arbitrary",)),
        name="castw",
    )(w_t, w_t)


def _inproj_kernel(x_ref, g_ref, w_ref, ws_ref, o_ref, os_ref, u_scr):
    @pl.when(pl.program_id(1) == 0)
    def _():
        u = _rms(x_ref[...], g_ref[...])
        u_hi = u.astype(BF16)
        u_scr[...] = u_hi
        u_lo = (u - u_hi.astype(F32)).astype(BF16)
        r = _dot_nt(u_hi, ws_ref[...])
        os_ref[...] = r[:, :LANES] + r[:, LANES:] + _dot_nt(u_lo, ws_ref[0:LANES, :])

    o_ref[...] = _dot_nt(u_scr[...], w_ref[...])


def _inproj(x, g, w_main, w_small, tm=1024, tn=1024):
    S, D = x.shape
    N = w_main.shape[0]
    return pl.pallas_call(
        _inproj_kernel,
        grid=(S // tm, N // tn),
        in_specs=[pl.BlockSpec((tm, D), lambda i, j: (i, 0)),
                  pl.BlockSpec((1, D), lambda i, j: (0, 0)),
                  pl.BlockSpec((tn, D), lambda i, j: (j, 0)),
                  pl.BlockSpec((2 * LANES, D), lambda i, j: (0, 0))],
        out_specs=[pl.BlockSpec((tm, tn), lambda i, j: (i, j)),
                   pl.BlockSpec((tm, LANES), lambda i, j: (i, 0))],
        out_shape=[jax.ShapeDtypeStruct((S, N), F32), jax.ShapeDtypeStruct((S, LANES), F32)],
        scratch_shapes=[pltpu.VMEM((tm, D), BF16)],
        compiler_params=_cparams(("parallel", "arbitrary")),
        name="inproj",
    )(x, g, w_main, w_small)


def _attn_kernel(slope_ref, q_ref, kp_ref, kc_ref, vp_ref, vc_ref, o_ref, kk, vv, o_scr, l_scr):
    i = pl.program_id(0)
    h = pl.program_id(1)
    T = ATTN_TILE
    B = ATTN_BLOCK
    slope = slope_ref[h]
    kk[0:T, :] = kp_ref[...]
    kk[T:2 * T, :] = kc_ref[...]
    vv[0:T, :] = vp_ref[...]
    vv[T:2 * T, :] = vc_ref[...]
    qi = lax.broadcasted_iota(I32, (B, 2 * B), 0)
    ki = lax.broadcasted_iota(I32, (B, 2 * B), 1)
    dist = qi + B - ki
    band = (dist >= 0) & (dist <= B)
    band_first = band & (ki >= B)
    distf = dist.astype(F32)
    log2e = float(np.log2(np.e))
    scale = HEAD_DIM ** -0.5 * log2e

    for p, (window, d) in enumerate(DILATED_PATTERNS):
        assert window // d == B and T % (B * d) == 0
        bias = jnp.where(band, distf * (-slope * d * log2e), NEG)
        bias_first = jnp.where(band_first, distf * (-slope * d * log2e), NEG)
        bias_j0 = jnp.where(i == 0, bias_first, bias)

        for b in range(T // B):
            r = b % d
            j = b // d
            qs = r + B * d * j
            ks = T - B * d + qs
            q = q_ref[pl.ds(qs, B, stride=d), :] * scale
            k = kk[pl.ds(ks, 2 * B, stride=d), :]
            v = vv[pl.ds(ks, 2 * B, stride=d), :]
            s = _dot_nt(q, k) + (bias_j0 if j == 0 else bias)
            m = jnp.max(s, axis=-1, keepdims=True)
            e = jnp.exp2(s - m)
            l = jnp.sum(e, axis=-1, keepdims=True)
            o = _dot(e, v) / l
            lse = m + jnp.log2(l)
            o_scr[p, pl.ds(qs, B, stride=d), :] = o
            l_scr[p, pl.ds(qs, B, stride=d), :] = jnp.broadcast_to(lse, (B, HEAD_DIM))

    l0, l1, l2 = l_scr[0], l_scr[1], l_scr[2]
    m = jnp.maximum(jnp.maximum(l0, l1), l2)
    w0, w1, w2 = jnp.exp2(l0 - m), jnp.exp2(l1 - m), jnp.exp2(l2 - m)
    o_ref[...] = (w0 * o_scr[0] + w1 * o_scr[1] + w2 * o_scr[2]) / (w0 + w1 + w2)


def _dilated_attention(proj, slopes):
    S = proj.shape[0]
    T = ATTN_TILE
    H = N_HEADS_A
    blk = (T, HEAD_DIM)
    return pl.pallas_call(
        _attn_kernel,
        grid_spec=pltpu.PrefetchScalarGridSpec(
            num_scalar_prefetch=1,
            grid=(S // T, H),
            in_specs=[pl.BlockSpec(blk, lambda i, h, s: (i, h)),
                      pl.BlockSpec(blk, lambda i, h, s: (jnp.maximum(i - 1, 0), H + h)),
                      pl.BlockSpec(blk, lambda i, h, s: (i, H + h)),
                      pl.BlockSpec(blk, lambda i, h, s: (jnp.maximum(i - 1, 0), 2 * H + h)),
                      pl.BlockSpec(blk, lambda i, h, s: (i, 2 * H + h))],
            out_specs=pl.BlockSpec(blk, lambda i, h, s: (i, h)),
            scratch_shapes=[pltpu.VMEM((2 * T, HEAD_DIM), F32), pltpu.VMEM((2 * T, HEAD_DIM), F32),
                            pltpu.VMEM((3, T, HEAD_DIM), F32), pltpu.VMEM((3, T, HEAD_DIM), F32)]),
        out_shape=jax.ShapeDtypeStruct((S, W_A), F32),
        compiler_params=_cparams(("parallel", "parallel")),
        name="attn",
    )(slopes, proj, proj, proj, proj, proj)


def _softplus(x):
    return jnp.maximum(x, 0.0) + jnp.log1p(jnp.exp(-jnp.abs(x)))


def _delta_kernel(hist_ref, qkv_ref, z_ref, ba_ref, cw_ref, alog_ref, dtb_ref, gout_ref, o_ref,
                  ext, act, gct, state):
    i = pl.program_id(0)
    CB = DELTA_BLOCK
    C = DELTA_CHUNK
    H = N_HEADS_B
    D = HEAD_DIM

    @pl.when(i == 0)
    def _():
        state[...] = jnp.zeros_like(state)
        ext[0:SUBLANES, :] = jnp.zeros((SUBLANES, 3 * W_B), F32)

    @pl.when(i > 0)
    def _():
        ext[0:SUBLANES, :] = hist_ref[...]

    ext[SUBLANES:SUBLANES + CB, :] = qkv_ref[...]

    for c in range(3 * H):
        sl = slice(c * D, (c + 1) * D)
        acc = cw_ref[CONV_WIDTH - 1:CONV_WIDTH, sl] * ext[SUBLANES:SUBLANES + CB, sl]
        for t in range(1, CONV_WIDTH):
            acc = acc + cw_ref[CONV_WIDTH - 1 - t:CONV_WIDTH - t, sl] * ext[SUBLANES - t:SUBLANES - t + CB, sl]
        a = acc * jax.nn.sigmoid(acc)
        if c < 2 * H:
            a = a * lax.rsqrt(jnp.sum(a * a, axis=-1, keepdims=True) + EPS)
        if c < H:
            a = a * (D ** -0.5)
        act[:, sl] = a

    ba = ba_ref[...]
    beta = jax.nn.sigmoid(ba)
    g = -jnp.exp(alog_ref[...]) * _softplus(ba + dtb_ref[...])
    row = lax.broadcasted_iota(I32, (CB, CB), 0)
    col = lax.broadcasted_iota(I32, (CB, CB), 1)
    chunk_tri = jnp.where((row // C == col // C) & (row >= col), 1.0, 0.0).astype(F32)
    gc = jnp.dot(chunk_tri, g, precision=HIGHEST, preferred_element_type=F32)
    gct[...] = gc.T
    eg = jnp.exp(gc)

    ri = lax.broadcasted_iota(I32, (C, C), 0)
    ci = lax.broadcasted_iota(I32, (C, C), 1)
    causal = ri >= ci
    strict = ri > ci
    gout = gout_ref[...]

    heads = range(H)
    for n in range(CB // C):
        rows = slice(n * C, (n + 1) * C)
        q = [act[rows, h * D:(h + 1) * D] for h in heads]
        k = [act[rows, (H + h) * D:(H + h + 1) * D] for h in heads]
        v = [act[rows, (2 * H + h) * D:(2 * H + h + 1) * D] for h in heads]
        beta_c = [beta[rows, h:h + 1] for h in heads]
        gc_c = [gc[rows, H + h:H + h + 1] for h in heads]
        eg_c = [eg[rows, H + h:H + h + 1] for h in heads]
        gc_r = [gct[H + h:H + h + 1, rows] for h in heads]
        g_last = [gc_r[h][:, C - 1:C] for h in heads]
        decay = [jnp.exp(jnp.where(causal, gc_c[h] - gc_r[h], NEG)) for h in heads]
        kb = [k[h] * beta_c[h] for h in heads]
        kq = [_dot_nt(jnp.concatenate([kb[h], q[h]], axis=0), k[h]) for h in heads]
        qk = [jnp.where(causal, kq[h][C:] * decay[h], 0.0) for h in heads]
        pw = [jnp.where(strict, -kq[h][:C] * decay[h], 0.0) for h in heads]
        r = pw
        pw = [_dot(pw[h], pw[h]) for h in heads]
        for _ in range(int(np.log2(C)) - 2):
            pr = [_dot(jnp.concatenate([pw[h], r[h]], axis=0), pw[h]) for h in heads]
            r = [r[h] + pw[h] + pr[h][C:] for h in heads]
            pw = [pr[h][:C] for h in heads]
        r = [r[h] + pw[h] + _dot(r[h], pw[h]) for h in heads]
        bmat = [jnp.concatenate([v[h] * beta_c[h], kb[h] * eg_c[h]], axis=-1) for h in heads]
        uw = [bmat[h] + _dot(r[h], bmat[h]) for h in heads]
        wq = [jnp.concatenate([uw[h][:, D:], q[h] * eg_c[h]], axis=0) for h in heads]
        kd = [k[h] * jnp.exp(g_last[h] - gc_c[h]) for h in heads]
        s_old = [state[h] for h in heads]
        ws = [_dot(wq[h], s_old[h]) for h in heads]
        v_new = [uw[h][:, :D] - ws[h][:C] for h in heads]
        for h in heads:
            state[h] = s_old[h] * jnp.exp(g_last[h]) + _dot_tn(kd[h], v_new[h])
        o = [ws[h][C:] + _dot(qk[h], v_new[h]) for h in heads]
        for h in heads:
            z = z_ref[rows, h * D:(h + 1) * D]
            o_ref[rows, h * D:(h + 1) * D] = _rms(o[h], gout) * (z * jax.nn.sigmoid(z))


def _gated_deltanet(proj, ba, conv_w, alog_pad, dtb_pad, g_out):
    S = proj.shape[0]
    CB = DELTA_BLOCK
    qkv_col = W_A * 3 // (3 * W_B)
    z_col = (3 * W_A + 3 * W_B) // W_B
    return pl.pallas_call(
        _delta_kernel,
        grid=(S // CB,),
        in_specs=[pl.BlockSpec((SUBLANES, 3 * W_B), lambda i: (jnp.maximum(i * (CB // SUBLANES) - 1, 0), qkv_col)),
                  pl.BlockSpec((CB, 3 * W_B), lambda i: (i, qkv_col)),
                  pl.BlockSpec((CB, W_B), lambda i: (i, z_col)),
                  pl.BlockSpec((CB, LANES), lambda i: (i, 0)),
                  pl.BlockSpec((CONV_WIDTH, 3 * W_B), lambda i: (0, 0)),
                  pl.BlockSpec((1, LANES), lambda i: (0, 0)),
                  pl.BlockSpec((1, LANES), lambda i: (0, 0)),
                  pl.BlockSpec((1, HEAD_DIM), lambda i: (0, 0))],
        out_specs=pl.BlockSpec((CB, W_B), lambda i: (i, 0)),
        out_shape=jax.ShapeDtypeStruct((S, W_B), F32),
        scratch_shapes=[pltpu.VMEM((CB + SUBLANES, 3 * W_B), F32),
                        pltpu.VMEM((CB, 3 * W_B), F32),
                        pltpu.VMEM((LANES, CB), F32),
                        pltpu.VMEM((N_HEADS_B, HEAD_DIM, HEAD_DIM), F32)],
        compiler_params=_cparams(("arbitrary",)),
        name="delta",
    )(proj, proj, proj, ba, conv_w, alog_pad, dtb_pad, g_out)


def _memkv_kernel(mem_ref, g_ref, w_ref, k_ref, v_ref):
    kv = _dot(_rms(mem_ref[...], g_ref[...]), w_ref[...])
    k_ref[...] = kv[:, :W_MEM].astype(BF16)
    v_ref[...] = kv[:, W_MEM:].astype(BF16)


def _memkv(mem, g, w_kv):
    M = mem.shape[0]
    return pl.pallas_call(
        _memkv_kernel,
        out_shape=[jax.ShapeDtypeStruct((M, W_MEM), BF16), jax.ShapeDtypeStruct((M, W_MEM), BF16)],
        compiler_params=pltpu.CompilerParams(vmem_limit_bytes=VMEM_LIMIT),
        name="memkv",
    )(mem, g, w_kv)


def _mix_kernel(attn_ref, delta_ref, x_ref, gattn_ref, wout_ref, gcross_ref, wq_ref, km_ref, vm_ref, wo_ref,
                gmoe_ref, wr_ref, br_ref, h2_ref, hn_ref, eid_ref, wts_ref, cnt_ref):
    an = _rms(attn_ref[...], gattn_ref[...])
    mix = jnp.concatenate([an.astype(BF16), delta_ref[...].astype(BF16)], axis=-1)
    h1 = x_ref[...] + jnp.dot(mix, wout_ref[...], preferred_element_type=F32)

    q = _dot(_rms(h1, gcross_ref[...]), wq_ref[...]) * (HEAD_DIM ** -0.5)
    outs = []
    for hh in range(N_MEM_HEADS):
        sl = slice(hh * HEAD_DIM, (hh + 1) * HEAD_DIM)
        s = _dot_nt(q[:, sl], km_ref[:, sl])
        e = jnp.exp(s - jnp.max(s, axis=-1, keepdims=True))
        outs.append(_dot(e, vm_ref[:, sl]) / jnp.sum(e, axis=-1, keepdims=True))
    h2 = h1 + _dot(jnp.concatenate(outs, axis=-1), wo_ref[...])
    h2_ref[...] = h2

    hn = _rms(h2, gmoe_ref[...])
    hn_ref[...] = hn
    logits = _dot_x3(hn, wr_ref) + br_ref[...]
    lane = lax.broadcasted_iota(I32, logits.shape, 1)
    gl = jnp.where(lane < N_GROUPS, logits, NEG)
    gmax = jnp.max(gl, axis=-1, keepdims=True)
    g_sel = jnp.min(jnp.where(gl == gmax, lane, LANES), axis=-1, keepdims=True)
    g_gate = 1.0 / jnp.sum(jnp.exp(gl - gmax), axis=-1, keepdims=True)
    in_group = (lane >= N_GROUPS) & ((lane - N_GROUPS) // EXPERTS_PER_GROUP == g_sel)
    el = jnp.where(in_group, logits, NEG)
    v1 = jnp.max(el, axis=-1, keepdims=True)
    i1 = jnp.min(jnp.where(in_group & (el == v1), lane, LANES), axis=-1, keepdims=True)
    in_rest = in_group & (lane != i1)
    el2 = jnp.where(in_rest, logits, NEG)
    v2 = jnp.max(el2, axis=-1, keepdims=True)
    i2 = jnp.min(jnp.where(in_rest & (el2 == v2), lane, LANES), axis=-1, keepdims=True)
    e2 = jnp.exp(v2 - v1)
    w1 = g_gate / (1.0 + e2)
    w2 = g_gate * e2 / (1.0 + e2)
    eid_ref[...] = jnp.where(lane == 0, i1 - N_GROUPS, jnp.where(lane == 1, i2 - N_GROUPS, 0))
    wts_ref[...] = jnp.where(lane == 0, w1, jnp.where(lane == 1, w2, 0.0))

    @pl.when(pl.program_id(0) == 0)
    def _():
        cnt_ref[...] = jnp.zeros_like(cnt_ref)

    picked = (lane == i1 - N_GROUPS) | (lane == i2 - N_GROUPS)
    cnt_ref[...] = cnt_ref[...] + jnp.sum(jnp.where(picked, 1.0, 0.0), axis=0, keepdims=True)


def _mix(attn, delta, x, g_attn, w_out, g_cross, w_q, k_mem, v_mem, w_o, g_moe, w_router, b_router, tm=512):
    S, D = x.shape
    row = lambda w: pl.BlockSpec((tm, w), lambda i: (i, 0))
    full = lambda a: pl.BlockSpec(a.shape, lambda i: (0, 0), pipeline_mode=pl.Buffered(1))
    return pl.pallas_call(
        _mix_kernel,
        grid=(S // tm,),
        in_specs=[row(W_A), row(W_B), row(D), full(g_attn), full(w_out), full(g_cross), full(w_q),
                  full(k_mem), full(v_mem), full(w_o), full(g_moe), full(w_router), full(b_router)],
        out_specs=[row(D), row(D), row(LANES), row(LANES), pl.BlockSpec((SUBLANES, LANES), lambda i: (0, 0))],
        out_shape=[jax.ShapeDtypeStruct((S, D), F32), jax.ShapeDtypeStruct((S, D), F32),
                   jax.ShapeDtypeStruct((S, LANES), I32), jax.ShapeDtypeStruct((S, LANES), F32),
                   jax.ShapeDtypeStruct((SUBLANES, LANES), F32)],
        compiler_params=_cparams(("arbitrary",)),
        name="mix",
    )(attn, delta, x, g_attn, w_out, g_cross, w_q, k_mem, v_mem, w_o, g_moe, w_router, b_router)


def _rank_kernel(eid_ref, cnt_ref, dest_ref, meta_ref, tab_ref, carry, pstart):
    i = pl.program_id(0)
    tm = eid_ref.shape[0]
    lane = lax.broadcasted_iota(I32, (tm, LANES), 1)
    e = eid_ref[...]
    e1 = e[:, 0:1]
    e2 = e[:, 1:2]
    onehot = jnp.where((lane == e1) | (lane == e2), 1.0, 0.0).astype(F32)

    @pl.when(i == 0)
    def _():
        cnt = cnt_ref[...]
        lane8 = lax.broadcasted_iota(I32, cnt.shape, 1)
        is_expert = lane8 < N_EXPERTS

        def prefix_sum(x):
            shift = 1
            while shift < LANES:
                x = x + jnp.where(lane8 >= shift, pltpu.roll(x, shift, 1), 0.0)
                shift *= 2
            return x

        padded = jnp.floor((cnt + (MOE_BLOCK - 1)) * (1.0 / MOE_BLOCK)) * MOE_BLOCK
        pend = prefix_sum(padded)
        pstart[...] = pend - padded
        carry[...] = jnp.zeros_like(carry)

        nb = meta_ref.shape[0]
        blane = lax.broadcasted_iota(I32, (nb, LANES), 1)
        brow = (lax.broadcasted_iota(I32, (nb, LANES), 0) * MOE_BLOCK).astype(F32)
        owner = jnp.sum(jnp.where((blane < N_EXPERTS) & (brow >= pend[0:1, :]), 1.0, 0.0), axis=-1, keepdims=True)
        owner = jnp.minimum(owner, N_EXPERTS - 1.0)
        seg_end = jnp.sum(jnp.where(blane.astype(F32) == owner, (pend - padded + cnt)[0:1, :], 0.0),
                          axis=-1, keepdims=True)
        valid = jnp.clip(seg_end - brow, 0.0, MOE_BLOCK * 1.0)
        meta_ref[...] = jnp.where(blane == 0, owner, jnp.where(blane == 1, valid, 0.0)).astype(I32)

        present = is_expert & (cnt > 0.0)
        order = prefix_sum(jnp.where(present, 1.0, 0.0)) - 1.0
        parity = order - 2.0 * jnp.floor(order * 0.5)
        nxt = jnp.where(present, lane8.astype(F32), N_EXPERTS * 1.0)
        nxt = jnp.where(lane8 < LANES - 1, pltpu.roll(nxt, LANES - 1, 1), N_EXPERTS * 1.0)
        shift = 1
        while shift < LANES:
            nxt = jnp.minimum(nxt, jnp.where(lane8 < LANES - shift, pltpu.roll(nxt, LANES - shift, 1), N_EXPERTS * 1.0))
            shift *= 2
        nxt = jnp.where(nxt >= N_EXPERTS, -1.0, nxt)
        sub8 = lax.broadcasted_iota(I32, cnt.shape, 0)
        first_block = (pend - padded) * (1.0 / MOE_BLOCK)
        rows = (parity, nxt, first_block)
        tab = jnp.zeros_like(cnt)
        for k, v in enumerate(rows):
            tab = jnp.where(sub8 == k, v, tab)
        tab_ref[...] = tab.astype(I32)

    r = lax.broadcasted_iota(I32, (tm, tm), 0)
    c = lax.broadcasted_iota(I32, (tm, tm), 1)
    before = jnp.where(r > c, 1.0, 0.0).astype(BF16)
    pos = jnp.dot(before, onehot.astype(BF16), preferred_element_type=F32) + carry[0:1, :] + pstart[0:1, :]
    d1 = jnp.sum(jnp.where(lane == e1, pos, 0.0), axis=-1, keepdims=True)
    d2 = jnp.sum(jnp.where(lane == e2, pos, 0.0), axis=-1, keepdims=True)
    dest = jnp.where(lane == 0, d1, jnp.where(lane == 1, d2, 0.0))
    dest_ref[...] = dest.T[:SUBLANES].astype(I32)
    carry[...] = carry[...] + jnp.sum(onehot, axis=0, keepdims=True)


def _rank(eid, cnt, n_blocks, tm=256):
    S = eid.shape[0]
    const = lambda shape: pl.BlockSpec(shape, lambda i: (0, 0))
    return pl.pallas_call(
        _rank_kernel,
        grid=(S // tm,),
        in_specs=[pl.BlockSpec((tm, LANES), lambda i: (i, 0)), const((SUBLANES, LANES))],
        out_specs=[pl.BlockSpec((SUBLANES, tm), lambda i: (0, i)), const((n_blocks, LANES)), const((SUBLANES, LANES))],
        out_shape=[jax.ShapeDtypeStruct((SUBLANES, S), I32), jax.ShapeDtypeStruct((n_blocks, LANES), I32),
                   jax.ShapeDtypeStruct((SUBLANES, LANES), I32)],
        scratch_shapes=[pltpu.VMEM((SUBLANES, LANES), F32), pltpu.VMEM((SUBLANES, LANES), F32)],
        compiler_params=_cparams(("arbitrary",)),
        name="rank",
    )(eid, cnt)


def _scatter_kernel(dest_ref, nval_ref, hn_ref, xb_ref, zeros, sem):
    i = pl.program_id(0)
    tm = hn_ref.shape[0]

    @pl.when(i == 0)
    def _():
        zeros[...] = jnp.zeros_like(zeros)

        def zero_block(b):
            return pltpu.make_async_copy(zeros, xb_ref.at[pl.ds(b * MOE_BLOCK, MOE_BLOCK), :], sem)

        def zissue(b, carry):
            @pl.when(nval_ref[b] < MOE_BLOCK)
            def _():
                zero_block(b).start()
            return carry

        def zdrain(b, carry):
            @pl.when(nval_ref[b] < MOE_BLOCK)
            def _():
                zero_block(b).wait()
            return carry

        lax.fori_loop(0, nval_ref.shape[0], zissue, 0)
        lax.fori_loop(0, nval_ref.shape[0], zdrain, 0)

    n_tok = pl.num_programs(0) * tm

    def issue(r, carry):
        for k in range(TOP_K):
            d = dest_ref[k * n_tok + i * tm + r]
            pltpu.make_async_copy(hn_ref.at[pl.ds(r, 1), :], xb_ref.at[pl.ds(d, 1), :], sem).start(priority=k)
        return carry

    lax.fori_loop(0, tm, issue, 0, unroll=ROW_DMA_UNROLL)
    for k in range(TOP_K):
        pltpu.make_async_copy(hn_ref, xb_ref.at[pl.ds(0, tm), :], sem).wait()


def _scatter_rows(dest_flat, block_valid, hn, n_rows, tm=512):
    S, D = hn.shape
    return pl.pallas_call(
        _scatter_kernel,
        grid_spec=pltpu.PrefetchScalarGridSpec(
            num_scalar_prefetch=2,
            grid=(S // tm,),
            in_specs=[pl.BlockSpec((tm, D), lambda i, d, nv: (i, 0))],
            out_specs=pl.BlockSpec(memory_space=pl.ANY),
            scratch_shapes=[pltpu.VMEM((MOE_BLOCK, D), F32), pltpu.SemaphoreType.DMA(())]),
        out_shape=jax.ShapeDtypeStruct((n_rows, D), F32),
        compiler_params=_cparams(("arbitrary",)),
        name="scatter",
    )(dest_flat, block_valid, hn)


def _expert_kernel(bexp_ref, nval_ref, slot_ref, next_ref, first_ref, x_ref, wg_ref, wu_ref, wd_ref, y_ref,
                   wg_f, wu_f, wd_f, wg_s, wu_s, wd_s, sem):
    b = pl.program_id(0)
    nb = pl.num_programs(0)
    e = bexp_ref[b]
    e_prev = bexp_ref[jnp.maximum(b - 1, 0)]
    nval = nval_ref[b]

    def weight_copy(w, ex, slot):
        src, dst = ((wg_ref, wg_f), (wu_ref, wu_f), (wd_ref, wd_f))[w]
        return pltpu.make_async_copy(src.at[ex], dst.at[slot], sem.at[slot, w])

    @pl.when(jnp.logical_and(b == 0, nval > 0))
    def _():
        for w in range(3):
            weight_copy(w, e, slot_ref[e]).start()

    @pl.when(jnp.logical_and(nval > 0, jnp.logical_or(b == 0, e != e_prev)))
    def _():
        slot = slot_ref[e]
        for w in range(3):
            weight_copy(w, e, slot).wait()
        wg_s[...] = wg_f[slot].astype(BF16)
        wu_s[...] = wu_f[slot].astype(BF16)
        wd_s[...] = wd_f[slot].astype(BF16)

    @pl.when(nval == 0)
    def _():
        y_ref[...] = jnp.zeros_like(y_ref)

    @pl.when(nval > 0)
    def _():
        nxt = next_ref[e]
        j = b - first_ref[e]
        b1 = jnp.minimum(b + 1, nb - 1)
        last = jnp.logical_or(b + 1 == nb, jnp.logical_or(bexp_ref[b1] != e, nval_ref[b1] == 0))
        for w in range(3):
            @pl.when(jnp.logical_and(nxt >= 0, jnp.logical_or(j == w, jnp.logical_and(last, j < w))))
            def _(w=w):
                weight_copy(w, nxt, 1 - slot_ref[e]).start()

        x = x_ref[...].astype(BF16)
        gate = jnp.dot(x, wg_s[...], preferred_element_type=F32)
        up = jnp.dot(x, wu_s[...], preferred_element_type=F32)
        hid = gate * jax.nn.sigmoid(gate) * up
        y_ref[...] = jnp.dot(hid.astype(BF16), wd_s[...], preferred_element_type=F32)


def _experts(block_expert, block_valid, expert_slot, expert_next, expert_first, xb, w_gate, w_up, w_down):
    P, D = xb.shape
    DE = w_gate.shape[-1]
    row_blk = pl.BlockSpec((MOE_BLOCK, D), lambda b, *_: (b, 0))
    hbm = pl.BlockSpec(memory_space=pl.ANY)
    return pl.pallas_call(
        _expert_kernel,
        grid_spec=pltpu.PrefetchScalarGridSpec(
            num_scalar_prefetch=5,
            grid=(P // MOE_BLOCK,),
            in_specs=[row_blk, hbm, hbm, hbm],
            out_specs=row_blk,
            scratch_shapes=[pltpu.VMEM((2, D, DE), F32), pltpu.VMEM((2, D, DE), F32), pltpu.VMEM((2, DE, D), F32),
                            pltpu.VMEM((D, DE), BF16), pltpu.VMEM((D, DE), BF16), pltpu.VMEM((DE, D), BF16),
                            pltpu.SemaphoreType.DMA((2, 3))]),
        out_shape=jax.ShapeDtypeStruct((P, D), F32),
        compiler_params=_cparams(("arbitrary",)),
        name="experts",
    )(block_expert, block_valid, expert_slot, expert_next, expert_first, xb, w_gate, w_up, w_down)


def _combine_kernel(dest_ref, h2_ref, wts_ref, g_ref, yb_ref, o_ref, buf, sem):
    i = pl.program_id(0)
    n = pl.num_programs(0)
    tm = h2_ref.shape[0]
    n_tok = n * tm

    def gather_block(blk, slot):
        def issue(r, carry):
            for k in range(TOP_K):
                d = dest_ref[k * n_tok + blk * tm + r]
                pltpu.make_async_copy(yb_ref.at[pl.ds(d, 1), :], buf.at[slot, k, pl.ds(r, 1), :],
                                      sem.at[slot]).start(priority=k)
            return carry

        lax.fori_loop(0, tm, issue, 0, unroll=ROW_DMA_UNROLL)

    @pl.when(i == 0)
    def _():
        gather_block(0, 0)

    @pl.when(i + 1 < n)
    def _():
        gather_block(i + 1, (i + 1) % 2)

    slot = i % 2
    for k in range(TOP_K):
        pltpu.make_async_copy(yb_ref.at[pl.ds(0, tm), :], buf.at[slot, k], sem.at[slot]).wait()
    wts = wts_ref[...]
    h = h2_ref[...] + wts[:, 0:1] * buf[slot, 0] + wts[:, 1:2] * buf[slot, 1]
    o_ref[...] = _rms(h, g_ref[...])


def _combine(dest_flat, h2, wts, g_final, yb, tm=256):
    S, D = h2.shape
    return pl.pallas_call(
        _combine_kernel,
        grid_spec=pltpu.PrefetchScalarGridSpec(
            num_scalar_prefetch=1,
            grid=(S // tm,),
            in_specs=[pl.BlockSpec((tm, D), lambda i, d: (i, 0)),
                      pl.BlockSpec((tm, LANES), lambda i, d: (i, 0)),
                      pl.BlockSpec((1, D), lambda i, d: (0, 0)),
                      pl.BlockSpec(memory_space=pl.ANY)],
            out_specs=pl.BlockSpec((tm, D), lambda i, d: (i, 0)),
            scratch_shapes=[pltpu.VMEM((2, TOP_K, tm, D), F32), pltpu.SemaphoreType.DMA((2,))]),
        out_shape=jax.ShapeDtypeStruct((S, D), F32),
        compiler_params=_cparams(("arbitrary",)),
        name="combine",
    )(dest_flat, h2, wts, g_final, yb)


def _lane_pad(v, offset):
    return jnp.zeros((1, LANES), F32).at[0, offset:offset + v.shape[0]].set(v.astype(F32))


def _layer(h, mem, g_mix, w_in, conv_w, a_log, dt_bias, g_delta_out, g_attn_out, w_out, g_cross, g_mem,
           w_q_mem, w_kv_mem, w_o_mem, g_moe, w_group, b_group, w_expert, b_expert, w_gate, w_up, w_down,
           g_final):
    S, D = h.shape
    H = N_HEADS_B
    w_main, w_small = _cast_w_in(w_in.T)
    proj, ba = _inproj(h, g_mix[None], w_main, w_small)

    slopes = jnp.asarray(2.0 ** (-8.0 * np.arange(1, N_HEADS_A + 1) / N_HEADS_A), dtype=F32)
    attn = _dilated_attention(proj, slopes)
    delta = _gated_deltanet(proj, ba, conv_w, _lane_pad(a_log, H), _lane_pad(dt_bias, H), g_delta_out[None])

    k_mem, v_mem = _memkv(mem, g_mem[None], w_kv_mem.astype(BF16))
    w_router = _split_hi_lo(jnp.pad(jnp.concatenate([w_group, w_expert], axis=1),
                                    ((0, 0), (0, LANES - N_GROUPS - N_EXPERTS))))
    b_router = _lane_pad(jnp.concatenate([b_group, b_expert]), 0)
    h2, hn, eid, wts, cnt = _mix(attn, delta, h, g_attn_out[None], w_out.astype(BF16), g_cross[None],
                                 w_q_mem.astype(BF16), k_mem, v_mem, w_o_mem.astype(BF16), g_moe[None],
                                 w_router, b_router)

    n_rows = S * TOP_K + N_EXPERTS * MOE_BLOCK
    dest, meta, tab = _rank(eid, cnt, n_rows // MOE_BLOCK)
    dest_flat = dest[:TOP_K].reshape(TOP_K * S)
    xb = _scatter_rows(dest_flat, meta[:, 1], hn, n_rows)
    yb = _experts(meta[:, 0], meta[:, 1], tab[0, :N_EXPERTS], tab[1, :N_EXPERTS], tab[2, :N_EXPERTS], xb,
                  w_gate, w_up, w_down)
    return _combine(dest_flat, h2, wts, g_final[None], yb)


def kernel(x, mem, g_mix, w_in, conv_w, a_log, dt_bias, g_delta_out, g_attn_out, w_out, g_cross, g_mem, w_q_mem,
           w_kv_mem, w_o_mem, g_moe, w_group, b_group, w_expert, b_expert, w_gate, w_up, w_down, g_final):
    assert x.shape[0] == 1 and mem.shape[0] == 1 and g_mix.shape[0] == 1
    out = _layer(x[0].astype(F32), mem[0].astype(F32), g_mix[0], w_in[0], conv_w[0], a_log[0], dt_bias[0],
                 g_delta_out[0], g_attn_out[0], w_out[0], g_cross[0], g_mem[0], w_q_mem[0], w_kv_mem[0],
                 w_o_mem[0], g_moe[0], w_group[0], b_group[0], w_expert[0], b_expert[0], w_gate[0], w_up[0],
                 w_down[0], g_final)
    return out[None].astype(x.dtype)
```
